```python
import jax, jax.numpy as jnp
from jax import lax
import numpy as np

D_MODEL = 1024
BATCH = 8
SEQ = 2048
DEPTH = 2
DEC_BATCH = 32
DEC_SEQ = 4
PAST_LEN = 8192
PAGE_SIZE = 128

N_HEADS_A = 8
HEAD_DIM_A = 64
MOBA_BLOCK = 256
MOBA_TOPK = 3
Q_CHUNK = 16
N_HEADS_R = 4
KEY_DIM_R = 128
VAL_DIM_R = 128
RET_CHUNK = 128
D_FF = 2816
EPS = 1e-6
NEG_INF = -1e30

W_A = N_HEADS_A * HEAD_DIM_A
W_RK = N_HEADS_R * KEY_DIM_R
W_RV = N_HEADS_R * VAL_DIM_R
SPLITS = (W_A, W_A, W_A, W_RK, W_RK, W_RV, W_RV, D_MODEL, D_MODEL)
D_IN = W_A * 3 + W_RK * 2 + W_RV * 2 + D_MODEL * 2

kernel_name = 'moba_retention_macaron_decode_step'


def rmsnorm(x, g):
    xf = x.astype(jnp.float32)
    y = xf * lax.rsqrt(jnp.mean(xf * xf, axis=-1, keepdims=True) + EPS)
    return (y * g.astype(jnp.float32)).astype(x.dtype)


def swiglu(x, w1, w3, w2):
    return (jax.nn.silu(x @ w1) * (x @ w3)) @ w2


def alibi_slopes():
    return 2.0 ** (-8.0 * (jnp.arange(N_HEADS_A, dtype=jnp.float32) + 1.0) / N_HEADS_A)


def retention_log_decay():
    return jnp.log(1.0 - 2.0 ** (-5.0 - jnp.arange(N_HEADS_R, dtype=jnp.float32)))


def to_blocks(k, v):
    B, L, H, Dh = k.shape
    pad = (-L) % MOBA_BLOCK
    k = jnp.pad(k, ((0, 0), (0, pad), (0, 0), (0, 0)))
    v = jnp.pad(v, ((0, 0), (0, pad), (0, 0), (0, 0)))
    nb = (L + pad) // MOBA_BLOCK
    kb = k.reshape(B, nb, MOBA_BLOCK, H, Dh).transpose(0, 3, 1, 2, 4)
    vb = v.reshape(B, nb, MOBA_BLOCK, H, Dh).transpose(0, 3, 1, 2, 4)
    kmean = jnp.mean(kb.astype(jnp.float32), axis=3)
    return kb, vb, kmean


def moba_core(q, q_pos, kb, vb, kmean):
    B, Sq, H, Dh = q.shape
    nb = kb.shape[2]
    n_top = min(MOBA_TOPK, nb)
    qf = q.astype(jnp.float32)
    own = q_pos // MOBA_BLOCK
    gate = jnp.einsum('bqhd,bhnd->bhqn', qf, kmean)
    fully_past = jnp.arange(nb)[None, :] < own[:, None]
    gate = jnp.where(fully_past[None, None], gate, NEG_INF)
    _, top_idx = lax.top_k(gate, n_top)
    top_ok = jnp.arange(n_top)[None, :] < own[:, None]
    own_idx = jnp.broadcast_to(own[None, None, :, None], (B, H, Sq, 1)).astype(top_idx.dtype)
    idx = jnp.concatenate([top_idx, own_idx], axis=-1)
    blk_ok = jnp.concatenate([top_ok, jnp.ones((Sq, 1), dtype=bool)], axis=-1)
    bi = jnp.arange(B)[:, None, None, None]
    hi = jnp.arange(H)[None, :, None, None]
    kg = kb[bi, hi, idx]
    vg = vb[bi, hi, idx]
    kpos = idx[..., None] * MOBA_BLOCK + jnp.arange(MOBA_BLOCK, dtype=idx.dtype)
    dist = q_pos.astype(idx.dtype)[None, None, :, None, None] - kpos
    s = jnp.einsum('bqhd,bhqnkd->bhqnk', qf, kg.astype(jnp.float32)) * (Dh ** -0.5)
    s = s - alibi_slopes()[None, :, None, None, None] * dist.astype(jnp.float32)
    s = jnp.where(blk_ok[None, None, :, :, None] & (dist >= 0), s, NEG_INF)
    ns = idx.shape[-1]
    p = jax.nn.softmax(s.reshape(B, H, Sq, ns * MOBA_BLOCK), axis=-1).reshape(s.shape)
    return jnp.einsum('bhqnk,bhqnkd->bqhd', p.astype(vg.dtype), vg)


def moba_prompt(q, k, v):
    B, S, H, Dh = q.shape
    kb, vb, kmean = to_blocks(k, v)
    nq = S // Q_CHUNK
    qc = q.reshape(B, nq, Q_CHUNK, H, Dh).transpose(1, 0, 2, 3, 4)
    pos = jnp.arange(S, dtype=jnp.int32).reshape(nq, Q_CHUNK)
    o = lax.map(lambda a: moba_core(a[0], a[1], kb, vb, kmean), (qc, pos))
    return o.transpose(1, 0, 2, 3, 4).reshape(B, S, H, Dh)


def moba_sample(q, k_new, v_new, cache_k_l, cache_v_l, page_table):
    Bd, n_pages = page_table.shape
    past = n_pages * cache_k_l.shape[1]
    H, Dh = cache_k_l.shape[2], cache_k_l.shape[3]
    k_past = cache_k_l[page_table].reshape(Bd, past, H, Dh)
    v_past = cache_v_l[page_table].reshape(Bd, past, H, Dh)
    k_all = jnp.concatenate([k_past, k_new.astype(k_past.dtype)], axis=1)
    v_all = jnp.concatenate([v_past, v_new.astype(v_past.dtype)], axis=1)
    kb, vb, kmean = to_blocks(k_all, v_all)
    pos = past + jnp.arange(q.shape[1], dtype=jnp.int32)
    return moba_core(q, pos, kb, vb, kmean)


def retention_chunk(state, q, k, v):
    q = q.astype(jnp.float32)
    k = k.astype(jnp.float32)
    v = v.astype(jnp.float32)
    C = q.shape[1]
    lg = retention_log_decay()
    i = jnp.arange(C, dtype=jnp.float32)
    diff = i[:, None] - i[None, :]
    dmat = jnp.where(diff[None] >= 0, jnp.exp(lg[:, None, None] * jnp.maximum(diff, 0.0)[None]), 0.0)
    inner = jnp.einsum('bihd,bjhd->bhij', q, k) * dmat[None]
    o = jnp.einsum('bhij,bjhe->bihe', inner, v)
    o = o + jnp.einsum('bihd,bhde->bihe', q, state) * jnp.exp(lg[None, :] * (i[:, None] + 1.0))[None, :, :, None]
    k_dec = k * jnp.exp(lg[None, :] * (C - 1.0 - i[:, None]))[None, :, :, None]
    new_state = jnp.exp(lg * C)[None, :, None, None] * state + jnp.einsum('bjhd,bjhe->bhde', k_dec, v)
    return new_state, o


def retention_prompt(q, k, v):
    B, S, H, Dk = q.shape
    Dv = v.shape[-1]
    nc = S // RET_CHUNK
    def chunks(a):
        return a.reshape(B, nc, RET_CHUNK, H, a.shape[-1]).transpose(1, 0, 2, 3, 4)
    s0 = jnp.zeros((B, H, Dk, Dv), jnp.float32)
    s_fin, o = lax.scan(lambda s, a: retention_chunk(s, a[0], a[1], a[2]), s0, (chunks(q), chunks(k), chunks(v)))
    return o.transpose(1, 0, 2, 3, 4).reshape(B, S, H, Dv), s_fin


def retention_sample(state, q, k, v):
    s_new, o = retention_chunk(state.astype(jnp.float32), q, k, v)
    return o, s_new


def head_groupnorm(o, g, b):
    mu = jnp.mean(o, axis=-1, keepdims=True)
    var = jnp.mean(jnp.square(o - mu), axis=-1, keepdims=True)
    y = ((o - mu) * lax.rsqrt(var + EPS)).reshape(o.shape[0], o.shape[1], -1)
    return y * g.astype(jnp.float32) + b.astype(jnp.float32)


def decoder_layer(x, attend, retain, g_ffn1, w1_ffn1, w3_ffn1, w2_ffn1, g_mix, w_in, g_q, g_k,
                  gn_gain, gn_bias, w_pa, w_pr, w_o, g_ffn2, w1_ffn2, w3_ffn2, w2_ffn2):
    B, S, _ = x.shape
    x = x + 0.5 * swiglu(rmsnorm(x, g_ffn1), w1_ffn1, w3_ffn1, w2_ffn1)
    h = rmsnorm(x, g_mix)
    z = h @ w_in
    qa, ka, va, qr, kr, vr, gr, gate_a, gate_r = jnp.split(z, np.cumsum(SPLITS)[:-1], axis=-1)
    qa = rmsnorm(qa.reshape(B, S, N_HEADS_A, HEAD_DIM_A), g_q)
    ka = rmsnorm(ka.reshape(B, S, N_HEADS_A, HEAD_DIM_A), g_k)
    va = va.reshape(B, S, N_HEADS_A, HEAD_DIM_A)
    oa = attend(qa, ka, va).reshape(B, S, W_A)
    qr = qr.reshape(B, S, N_HEADS_R, KEY_DIM_R)
    kr = kr.reshape(B, S, N_HEADS_R, KEY_DIM_R) * (KEY_DIM_R ** -0.5)
    vr = vr.reshape(B, S, N_HEADS_R, VAL_DIM_R)
    o_r, s_new = retain(qr, kr, vr)
    o_r = head_groupnorm(o_r, gn_gain, gn_bias).astype(x.dtype) * jax.nn.silu(gr)
    merged = jax.nn.sigmoid(gate_a) * (oa @ w_pa) + jax.nn.sigmoid(gate_r) * (o_r @ w_pr)
    x = x + merged @ w_o
    x = x + 0.5 * swiglu(rmsnorm(x, g_ffn2), w1_ffn2, w3_ffn2, w2_ffn2)
    return x, ka, va, s_new


def setup_inputs(seed: int = 0) -> dict:
    key = jax.random.key(seed)
    ks = jax.random.split(key, 32)
    f32 = jnp.float32
    n_pages = PAST_LEN // PAGE_SIZE
    n_used = DEC_BATCH * n_pages
    n_pool = n_used + max(1, n_used // 4)

    def nrm(k, shape, scale):
        return jax.random.normal(k, shape, f32) * scale

    def gain(k, shape):
        return 1.0 + 0.05 * jax.random.normal(k, shape, f32)

    perm = jax.random.permutation(ks[5], n_pool).astype(jnp.int32)
    return {
        'x_prompt': nrm(ks[0], (BATCH, SEQ, D_MODEL), 1.0),
        'x_sample': nrm(ks[1], (DEC_BATCH, DEC_SEQ, D_MODEL), 1.0),
        'cache_k': nrm(ks[2], (DEPTH, n_pool, PAGE_SIZE, N_HEADS_A, HEAD_DIM_A), 1.0),
        'cache_v': nrm(ks[3], (DEPTH, n_pool, PAGE_SIZE, N_HEADS_A, HEAD_DIM_A), 1.0),
        'state_ret': nrm(ks[4], (DEPTH, DEC_BATCH, N_HEADS_R, KEY_DIM_R, VAL_DIM_R), 1.0),
        'page_table': perm[:n_used].reshape(DEC_BATCH, n_pages),
        'g_ffn1': gain(ks[6], (DEPTH, D_MODEL)),
        'w1_ffn1': nrm(ks[7], (DEPTH, D_MODEL, D_FF), D_MODEL ** -0.5),
        'w3_ffn1': nrm(ks[8], (DEPTH, D_MODEL, D_FF), D_MODEL ** -0.5),
        'w2_ffn1': nrm(ks[9], (DEPTH, D_FF, D_MODEL), D_FF ** -0.5),
        'g_mix': gain(ks[10], (DEPTH, D_MODEL)),
        'w_in': nrm(ks[11], (DEPTH, D_MODEL, D_IN), D_MODEL ** -0.5),
        'g_q': gain(ks[12], (DEPTH, HEAD_DIM_A)),
        'g_k': gain(ks[13], (DEPTH, HEAD_DIM_A)),
        'gn_gain': gain(ks[14], (DEPTH, W_RV)),
        'gn_bias': nrm(ks[15], (DEPTH, W_RV), 0.02),
        'w_pa': nrm(ks[16], (DEPTH, W_A, D_MODEL), W_A ** -0.5),
        'w_pr': nrm(ks[17], (DEPTH, W_RV, D_MODEL), W_RV ** -0.5),
        'w_o': nrm(ks[18], (DEPTH, D_MODEL, D_MODEL), D_MODEL ** -0.5),
        'g_ffn2': gain(ks[19], (DEPTH, D_MODEL)),
        'w1_ffn2': nrm(ks[20], (DEPTH, D_MODEL, D_FF), D_MODEL ** -0.5),
        'w3_ffn2': nrm(ks[21], (DEPTH, D_MODEL, D_FF), D_MODEL ** -0.5),
        'w2_ffn2': nrm(ks[22], (DEPTH, D_FF, D_MODEL), D_FF ** -0.5),
    }


def reference(x_prompt, x_sample, cache_k, cache_v, state_ret, page_table,
              g_ffn1, w1_ffn1, w3_ffn1, w2_ffn1, g_mix, w_in, g_q, g_k, gn_gain, gn_bias,
              w_pa, w_pr, w_o, g_ffn2, w1_ffn2, w3_ffn2, w2_ffn2):
    xp = x_prompt
    xs = x_sample
    kp_l, vp_l, sp_l, ks_l, vs_l, ss_l = [], [], [], [], [], []
    for l in range(DEPTH):
        lw = (g_ffn1[l], w1_ffn1[l], w3_ffn1[l], w2_ffn1[l], g_mix[l], w_in[l], g_q[l], g_k[l],
              gn_gain[l], gn_bias[l], w_pa[l], w_pr[l], w_o[l], g_ffn2[l], w1_ffn2[l], w3_ffn2[l], w2_ffn2[l])
        xp, kp, vp, sp = decoder_layer(xp, moba_prompt, retention_prompt, *lw)
        attend_s = lambda q, k, v, ck=cache_k[l], cv=cache_v[l]: moba_sample(q, k, v, ck, cv, page_table)
        retain_s = lambda q, k, v, st=state_ret[l]: retention_sample(st, q, k, v)
        xs, kd, vd, sd = decoder_layer(xs, attend_s, retain_s, *lw)
        kp_l.append(kp); vp_l.append(vp); sp_l.append(sp)
        ks_l.append(kd); vs_l.append(vd); ss_l.append(sd)
    k_prompt = jnp.stack(kp_l)
    v_prompt = jnp.stack(vp_l)
    state_prompt = jnp.stack(sp_l)
    k_sample = jnp.stack(ks_l)
    v_sample = jnp.stack(vs_l)
    state_sample = jnp.stack(ss_l)
    return (xp, xs, k_prompt, v_prompt, state_prompt, k_sample, v_sample, state_sample)
```

```python
import functools

import jax
import jax.numpy as jnp
from jax import lax
from jax.experimental import pallas as pl
from jax.experimental.pallas import tpu as pltpu

F32 = jnp.float32
BF16 = jnp.bfloat16

D_MODEL = 1024
D_FF = 2816
N_HEADS_A = 8
HEAD_DIM_A = 64
MOBA_BLOCK = 256
MOBA_TOPK = 3
N_HEADS_R = 4
KEY_DIM_R = 128
VAL_DIM_R = 128
EPS = 1e-6
NEG_INF = -1e30
W_A = N_HEADS_A * HEAD_DIM_A
W_R = N_HEADS_R * KEY_DIM_R

LANES = 128
VMEM_LIMIT_BYTES = 56 * 1024 * 1024
SAMPLE_ROWS = 16
RET_CHUNK_PROMPT = 256
PAGES_PER_STEP = 8


def _nt(a, b):
    return lax.dot_general(a, b, (((1,), (1,)), ((), ())), preferred_element_type=F32)


def _tn(a, b):
    return lax.dot_general(a, b, (((0,), (0,)), ((), ())), preferred_element_type=F32)


def _mm(a, b):
    return jnp.dot(a, b, preferred_element_type=F32)


def _rms(x, g):
    return x * lax.rsqrt(jnp.mean(x * x, axis=-1, keepdims=True) + EPS) * g


def _split_bf16(x):
    hi = x.astype(BF16)
    lo = (x - hi.astype(F32)).astype(BF16)
    return hi, lo


def _params(*sem):
    return pltpu.CompilerParams(dimension_semantics=sem, vmem_limit_bytes=VMEM_LIMIT_BYTES)


def _row_spec(tm, cols):
    return pl.BlockSpec((tm, cols), lambda i: (i, 0))


def _const_spec(shape):
    return pl.BlockSpec(shape, lambda i: (0,) * len(shape), pipeline_mode=pl.Buffered(1))


def _ffn_body(x_ref, g_ref, w1_ref, w3_ref, w2_ref, o_ref):
    x = x_ref[...]
    h = _rms(x, g_ref[...]).astype(BF16)
    a = _mm(h, w1_ref[...])
    b = _mm(h, w3_ref[...])
    act = (a * jax.nn.sigmoid(a) * b).astype(BF16)
    o_ref[...] = x + 0.5 * _mm(act, w2_ref[...])


def _ffn(x, g, w1, w3, w2, tm):
    m = x.shape[0]
    return pl.pallas_call(
        _ffn_body,
        grid=(m // tm,),
        in_specs=[_row_spec(tm, D_MODEL), _const_spec((1, D_MODEL)),
                  _const_spec((D_MODEL, D_FF)), _const_spec((D_MODEL, D_FF)),
                  _const_spec((D_FF, D_MODEL))],
        out_specs=_row_spec(tm, D_MODEL),
        out_shape=jax.ShapeDtypeStruct((m, D_MODEL), F32),
        compiler_params=_params("parallel"),
        name="ffn",
    )(x, g, w1, w3, w2)


_IN_COLS = (W_A, W_A, W_A, W_R, W_R, W_R, W_R, D_MODEL, D_MODEL)
_IN_OFFS = tuple(sum(_IN_COLS[:i]) for i in range(len(_IN_COLS) + 1))
D_IN = _IN_OFFS[-1]


def _inproj_body(x_ref, g_ref, w_ref, gq_ref, gk_ref, e_ref,
                 q_o, kf_o, kb_o, vf_o, vb_o, qr_o, kr_o, vr_o, gr_o, ga_o, gb_o):
    h = _rms(x_ref[...], g_ref[...]).astype(BF16)

    def proj(i):
        return _mm(h, w_ref[:, _IN_OFFS[i]:_IN_OFFS[i + 1]])

    def head_rms(z, gain):
        ss = _mm((z * z).astype(BF16), e_ref[...])
        return z * lax.rsqrt(ss * (1.0 / HEAD_DIM_A) + EPS) * gain

    q_o[...] = head_rms(proj(0), gq_ref[...]) * (HEAD_DIM_A ** -0.5)
    k = head_rms(proj(1), gk_ref[...])
    kf_o[...] = k
    kb_o[...] = k.astype(BF16)
    v = proj(2)
    vf_o[...] = v
    vb_o[...] = v.astype(BF16)
    qr_o[...] = proj(3).astype(BF16)
    kr_o[...] = (proj(4) * (KEY_DIM_R ** -0.5)).astype(BF16)
    vr_o[...] = proj(5).astype(BF16)
    gr_o[...] = proj(6)
    ga_o[...] = proj(7)
    gb_o[...] = proj(8)


def _inproj(x, g, w, gq, gk, e, tm):
    m = x.shape[0]
    outs = [(W_A, F32), (W_A, F32), (W_A, BF16), (W_A, F32), (W_A, BF16),
            (W_R, BF16), (W_R, BF16), (W_R, BF16), (W_R, F32), (D_MODEL, F32), (D_MODEL, F32)]
    return pl.pallas_call(
        _inproj_body,
        grid=(m // tm,),
        in_specs=[_row_spec(tm, D_MODEL), _const_spec((1, D_MODEL)), _const_spec((D_MODEL, D_IN)),
                  _const_spec((1, W_A)), _const_spec((1, W_A)), _const_spec((W_A, W_A))],
        out_specs=[_row_spec(tm, c) for c, _ in outs],
        out_shape=[jax.ShapeDtypeStruct((m, c), dt) for c, dt in outs],
        compiler_params=_params("parallel"),
        name="inproj",
    )(x, g, w, gq, gk, e)


def _outproj_body(oa_ref, or_ref, ga_ref, gb_ref, x_ref, wpa_ref, wpr_ref, wo_ref, o_ref):
    a = _mm(oa_ref[...], wpa_ref[...])
    r = _mm(or_ref[...], wpr_ref[...])
    merged = jax.nn.sigmoid(ga_ref[...]) * a + jax.nn.sigmoid(gb_ref[...]) * r
    o_ref[...] = x_ref[...] + _mm(merged.astype(BF16), wo_ref[...])


def _outproj(oa, o_r, ga, gb, x, wpa, wpr, wo, tm):
    m = x.shape[0]
    return pl.pallas_call(
        _outproj_body,
        grid=(m // tm,),
        in_specs=[_row_spec(tm, W_A), _row_spec(tm, W_R), _row_spec(tm, D_MODEL),
                  _row_spec(tm, D_MODEL), _row_spec(tm, D_MODEL),
                  _const_spec((W_A, D_MODEL)), _const_spec((W_R, D_MODEL)),
                  _const_spec((D_MODEL, D_MODEL))],
        out_specs=_row_spec(tm, D_MODEL),
        out_shape=jax.ShapeDtypeStruct((m, D_MODEL), F32),
        compiler_params=_params("parallel"),
        name="outproj",
    )(oa, o_r, ga, gb, x, wpa, wpr, wo)


def _rank_select(gate, lane, n_valid_mask, candidates):
    gate = jnp.where(n_valid_mask, gate, NEG_INF)
    cnt = jnp.zeros(gate.shape, F32)
    for c in candidates:
        gc = gate[:, c:c + 1]
        beats = (gc > gate) | ((gc == gate) & (lane > c))
        cnt = cnt + jnp.where(beats, 1.0, 0.0)
    return jnp.where((cnt < MOBA_TOPK) & n_valid_mask, 1.0, 0.0)


def _moba_prompt_body(q_ref, k_ref, v_ref, k32_ref, cb_ref, o_ref, km_ref, *, nb):
    j = pl.program_id(2)
    blk = MOBA_BLOCK

    @pl.when(j == 0)
    def _():
        km_ref[...] = jnp.zeros_like(km_ref)
        km_ref[0:nb, :] = jnp.mean(k32_ref[0].reshape(nb, blk, LANES), axis=1)

    q2 = q_ref[0]
    lane = lax.broadcasted_iota(jnp.int32, (1, LANES), 1)
    row = lax.broadcasted_iota(jnp.int32, (blk, blk), 0)
    col = lax.broadcasted_iota(jnp.int32, (blk, blk), 1)
    causal = col <= row
    km_hi, km_lo = _split_bf16(km_ref[...])
    past = lane < j
    joff = pl.multiple_of(j * blk, blk)

    outs = []
    for hh in range(2):
        in_head = (lane >= HEAD_DIM_A * hh) & (lane < HEAD_DIM_A * (hh + 1))
        qh32 = jnp.where(in_head, q2, 0.0)
        q_hi, q_lo = _split_bf16(qh32)
        gate = _nt(q_hi, km_hi) + _nt(q_hi, km_lo) + _nt(q_lo, km_hi)
        sel = _rank_select(gate, lane, past, range(nb))

        def scores(off, i):
            s = _nt(q_hi, k_ref[0, pl.ds(off, blk), :])
            return s + cb_ref[0, i][hh:hh + 1, :]

        s = jnp.where(causal, scores(joff, j), NEG_INF)
        m = jnp.max(s, axis=-1, keepdims=True)
        p = jnp.exp(s - m)
        l = jnp.sum(p, axis=-1, keepdims=True)
        acc = _mm(p.astype(BF16), v_ref[0, pl.ds(joff, blk), :])

        def past_block(i, carry):
            m, l, acc = carry
            off = pl.multiple_of(i * blk, blk)
            sel_i = jnp.sum(jnp.where(lane == i, sel, 0.0), axis=-1, keepdims=True) > 0.5
            s = jnp.where(sel_i, scores(off, i), NEG_INF)
            m_new = jnp.maximum(m, jnp.max(s, axis=-1, keepdims=True))
            alpha = jnp.exp(m - m_new)
            p = jnp.exp(s - m_new)
            l = alpha * l + jnp.sum(p, axis=-1, keepdims=True)
            acc = alpha * acc + _mm(p.astype(BF16), v_ref[0, pl.ds(off, blk), :])
            return m_new, l, acc

        m, l, acc = lax.fori_loop(0, j, past_block, (m, l, acc))
        outs.append(acc / l)

    o_ref[0] = jnp.where(lane < HEAD_DIM_A, outs[0], outs[1]).astype(o_ref.dtype)


def _moba_prompt(q, kb, vb, kf, cb):
    b, s, _ = q.shape
    nb = s // MOBA_BLOCK
    npair = N_HEADS_A // 2
    kv_spec = pl.BlockSpec((1, s, LANES), lambda bi, p, j: (bi, 0, p))
    return pl.pallas_call(
        functools.partial(_moba_prompt_body, nb=nb),
        grid=(b, npair, nb),
        in_specs=[pl.BlockSpec((1, MOBA_BLOCK, LANES), lambda bi, p, j: (bi, j, p)),
                  kv_spec, kv_spec, kv_spec,
                  pl.BlockSpec((1, nb, 2, MOBA_BLOCK), lambda bi, p, j: (p, 0, 0, 0))],
        out_specs=pl.BlockSpec((1, MOBA_BLOCK, LANES), lambda bi, p, j: (bi, j, p)),
        out_shape=jax.ShapeDtypeStruct((b, s, W_A), BF16),
        scratch_shapes=[pltpu.VMEM((LANES, LANES), F32)],
        compiler_params=_params("parallel", "parallel", "arbitrary"),
        name="moba_prompt",
    )(q, kb, vb, kf, cb)


def _moba_sample_body(pt_ref, q_ref, kn_ref, vn_ref, *rest, n_real, past_len):
    npg = PAGES_PER_STEP
    kp = rest[0:npg]
    vp = rest[npg:2 * npg]
    o_ref = rest[2 * npg]
    s_ref, p_ref, g_ref, acc_ref, l_ref = rest[2 * npg + 1:]
    ph = pl.program_id(1)
    g = pl.program_id(2)
    n_groups = pl.num_programs(2)
    n_pages = s_ref.shape[0]
    rows = N_HEADS_A * n_real

    rowi = lax.broadcasted_iota(jnp.int32, (rows, W_A), 0)
    lanei = lax.broadcasted_iota(jnp.int32, (rows, W_A), 1)
    head_mask = (lanei >> 6) == (rowi & (N_HEADS_A - 1))
    q = q_ref[0]
    q_rep = jnp.concatenate(
        [jnp.broadcast_to(q[t:t + 1, :], (N_HEADS_A, W_A)) for t in range(n_real)], axis=0)
    q_hi, q_lo = _split_bf16(jnp.where(head_mask, q_rep, 0.0))
    q2 = jnp.concatenate([q_hi, q_lo], axis=0)

    def raw_scores(keys_t=None, keys=None):
        s2 = _mm(q2, keys_t) if keys_t is not None else _nt(q2, keys)
        return s2[:rows] + s2[rows:]

    lane = lax.broadcasted_iota(jnp.int32, (1, LANES), 1)

    @pl.when(ph == 0)
    def _():
        @pl.when(g == 0)
        def _():
            g_ref[...] = jnp.zeros_like(g_ref)

        gsum = g_ref[...]
        for n in range(npg):
            pg = g * npg + n
            s = raw_scores(keys_t=kp[n][0, 0].astype(BF16))
            s_ref[pg] = s
            gsum = jnp.where(lane == pg, jnp.sum(s, axis=-1, keepdims=True), gsum)
        g_ref[...] = gsum

    @pl.when((ph == 1) & (g == 0))
    def _():
        gsum = g_ref[...]
        gate = (gsum + pltpu.roll(gsum, LANES - 1, 1)) * (1.0 / MOBA_BLOCK)
        valid = ((lane & 1) == 0) & (lane < n_pages)
        sel = _rank_select(gate, lane, valid, range(0, n_pages, 2))

        rcol = lax.broadcasted_iota(jnp.int32, (rows, 1), 0)
        slope = jnp.exp2(-((rcol & (N_HEADS_A - 1)) + 1).astype(F32))
        qpos = past_len + (rcol >> 3)

        m = jnp.full((rows, 1), NEG_INF, F32)
        for pg in range(n_pages):
            c = 2 * (pg // 2)
            kpos = pg * LANES + lane
            s = s_ref[pg] - slope * (qpos - kpos).astype(F32)
            s = jnp.where(sel[:, c:c + 1] > 0.5, s, NEG_INF)
            s_ref[pg] = s
            m = jnp.maximum(m, jnp.max(s, axis=-1, keepdims=True))
        tnew = lax.broadcasted_iota(jnp.int32, (1, SAMPLE_ROWS), 1)
        dist = (rcol >> 3) - tnew
        s_own = raw_scores(keys=kn_ref[0]) - slope * dist.astype(F32)
        s_own = jnp.where(dist >= 0, s_own, NEG_INF)
        m = jnp.maximum(m, jnp.max(s_own, axis=-1, keepdims=True))

        l = jnp.zeros((rows, 1), F32)
        for pg in range(n_pages):
            p = jnp.exp(s_ref[pg] - m)
            l = l + jnp.sum(p, axis=-1, keepdims=True)
            p_ref[pg] = p.astype(BF16)
        p_own = jnp.exp(s_own - m)
        l = l + jnp.sum(p_own, axis=-1, keepdims=True)
        acc_ref[...] = _mm(p_own.astype(BF16), vn_ref[0])
        l_ref[...] = jnp.broadcast_to(l, l_ref.shape)

    @pl.when(ph == 1)
    def _():
        acc = acc_ref[...]
        for n in range(npg):
            acc = acc + _nt(p_ref[g * npg + n], vp[n][0, 0].astype(BF16))
        acc_ref[...] = acc

    @pl.when((ph == 1) & (g == n_groups - 1))
    def _():
        o = jnp.where(head_mask, acc_ref[...] / l_ref[:, 0:1], 0.0)
        out_rows = [jnp.sum(o[N_HEADS_A * t:N_HEADS_A * (t + 1), :], axis=0, keepdims=True)
                    for t in range(n_real)]
        out_rows.append(jnp.zeros((SAMPLE_ROWS - n_real, W_A), F32))
        o_ref[0] = jnp.concatenate(out_rows, axis=0).astype(o_ref.dtype)


def _moba_sample(page_table, q, kn, vn, cache_kt, cache_vt, layer, n_real):
    bd, n_pages = page_table.shape
    page = cache_kt.shape[-1]
    assert page == LANES and MOBA_BLOCK == 2 * page and n_pages % PAGES_PER_STEP == 0
    npg = PAGES_PER_STEP
    n_groups = n_pages // npg
    rows = N_HEADS_A * n_real

    def page_spec(n, is_k):
        def index_map(b, ph, g, pt):
            grp = jnp.where(ph == 0, g, n_groups - 1) if is_k else jnp.where(ph == 0, 0, g)
            return (layer, pt[b, grp * npg + n], 0, 0)
        return pl.BlockSpec((1, 1, W_A, page), index_map)

    tok_spec = pl.BlockSpec((1, SAMPLE_ROWS, W_A), lambda b, ph, g, pt: (b, 0, 0))
    grid_spec = pltpu.PrefetchScalarGridSpec(
        num_scalar_prefetch=1,
        grid=(bd, 2, n_groups),
        in_specs=[tok_spec, tok_spec, tok_spec]
        + [page_spec(n, True) for n in range(npg)] + [page_spec(n, False) for n in range(npg)],
        out_specs=tok_spec,
        scratch_shapes=[pltpu.VMEM((n_pages, rows, LANES), F32),
                        pltpu.VMEM((n_pages, rows, LANES), BF16),
                        pltpu.VMEM((rows, LANES), F32),
                        pltpu.VMEM((rows, W_A), F32),
                        pltpu.VMEM((rows, LANES), F32)],
    )
    return pl.pallas_call(
        functools.partial(_moba_sample_body, n_real=n_real, past_len=n_pages * page),
        grid_spec=grid_spec,
        out_shape=jax.ShapeDtypeStruct((bd, SAMPLE_ROWS, W_A), BF16),
        compiler_params=_params("parallel", "arbitrary", "arbitrary"),
        name="moba_sample",
    )(page_table, q, kn, vn, *([cache_kt] * npg), *([cache_vt] * npg))


def _retention_body(lg_ref, q_ref, k_ref, v_ref, gr_ref, gain_ref, bias_ref, s0_ref,
                    o_ref, s_out_ref, *, chunk, n_real, n_chunks):
    lg = lg_ref[0][:, 0:1]
    ii = lax.broadcasted_iota(jnp.int32, (chunk, chunk), 0)
    jj = lax.broadcasted_iota(jnp.int32, (chunk, chunk), 1)
    diff = (ii - jj).astype(F32)
    dmat = jnp.where(diff >= 0, jnp.exp(lg * jnp.maximum(diff, 0.0)), 0.0)
    icol = lax.broadcasted_iota(jnp.int32, (chunk, 1), 0).astype(F32)
    qdec = jnp.exp(lg * (icol + 1.0))
    kdec = jnp.where(icol < n_real, jnp.exp(lg * (n_real - 1.0 - icol)), 0.0)
    chunk_decay = jnp.exp(lg * float(n_real))
    gain = gain_ref[...]
    bias = bias_ref[...]

    def step(c, state):
        off = pl.multiple_of(c * chunk, chunk)
        qc = q_ref[0, pl.ds(off, chunk), :]
        kc = k_ref[0, pl.ds(off, chunk), :]
        vc = v_ref[0, pl.ds(off, chunk), :]
        inner = _nt(qc, kc) * dmat
        o = _mm(inner.astype(BF16), vc) + _mm(qc, state.astype(BF16)) * qdec
        kd = (kc.astype(F32) * kdec).astype(BF16)
        new_state = chunk_decay * state + _tn(kd, vc)
        mu = jnp.mean(o, axis=-1, keepdims=True)
        d = o - mu
        var = jnp.mean(d * d, axis=-1, keepdims=True)
        y = d * lax.rsqrt(var + EPS) * gain + bias
        gr = gr_ref[0, pl.ds(off, chunk), :]
        o_ref[0, pl.ds(off, chunk), :] = (y * (gr * jax.nn.sigmoid(gr))).astype(o_ref.dtype)
        return new_state

    s_out_ref[0, 0] = lax.fori_loop(0, n_chunks, step, s0_ref[0, 0])


def _retention(lg, q, k, v, gr, gain, bias, state0, chunk, n_real):
    b, s, _ = q.shape
    hd = KEY_DIM_R
    seq_spec = pl.BlockSpec((1, s, hd), lambda bi, h: (bi, 0, h))
    vec_spec = pl.BlockSpec((1, hd), lambda bi, h: (0, h))
    st_spec = pl.BlockSpec((1, 1, hd, hd), lambda bi, h: (bi, h, 0, 0))
    return pl.pallas_call(
        functools.partial(_retention_body, chunk=chunk, n_real=n_real, n_chunks=s // chunk),
        grid=(b, N_HEADS_R),
        in_specs=[pl.BlockSpec((1, 1, LANES), lambda bi, h: (h, 0, 0)),
                  seq_spec, seq_spec, seq_spec, seq_spec, vec_spec, vec_spec, st_spec],
        out_specs=[seq_spec, st_spec],
        out_shape=[jax.ShapeDtypeStruct((b, s, W_R), BF16),
                   jax.ShapeDtypeStruct((b, N_HEADS_R, hd, hd), F32)],
        compiler_params=_params("parallel", "parallel"),
        name="retention",
    )(lg, q, k, v, gr, gain, bias, state0)


def kernel(x_prompt, x_sample, cache_k, cache_v, state_ret, page_table, g_ffn1, w1_ffn1, w3_ffn1,
           w2_ffn1, g_mix, w_in, g_q, g_k, gn_gain, gn_bias, w_pa, w_pr, w_o, g_ffn2, w1_ffn2,
           w3_ffn2, w2_ffn2):
    b, s, _ = x_prompt.shape
    bd, sd, _ = x_sample.shape
    depth = w_in.shape[0]
    n_pool, page = cache_k.shape[1], cache_k.shape[2]
    mp, ms = b * s, bd * SAMPLE_ROWS
    tm_p, tm_s = 512, ms

    xp = x_prompt.reshape(mp, D_MODEL)
    xs = jnp.pad(x_sample, ((0, 0), (0, SAMPLE_ROWS - sd), (0, 0))).reshape(ms, D_MODEL)

    cache_kt = jnp.transpose(cache_k, (0, 1, 3, 4, 2)).reshape(depth, n_pool, W_A, page)
    cache_vt = jnp.transpose(cache_v, (0, 1, 3, 4, 2)).reshape(depth, n_pool, W_A, page)

    heads = jnp.arange(N_HEADS_A, dtype=F32)
    slopes = 2.0 ** (-8.0 * (heads + 1.0) / N_HEADS_A)
    kpos = jnp.arange(s, dtype=F32).reshape(s // MOBA_BLOCK, 1, MOBA_BLOCK)
    col_bias = (slopes.reshape(N_HEADS_A // 2, 1, 2, 1) * kpos[None]).astype(F32)
    log_decay = jnp.log(1.0 - 2.0 ** (-5.0 - jnp.arange(N_HEADS_R, dtype=F32)))
    lg = jnp.broadcast_to(log_decay.reshape(N_HEADS_R, 1, 1), (N_HEADS_R, 1, LANES))
    head_ones = jnp.kron(jnp.eye(N_HEADS_A, dtype=F32),
                         jnp.ones((HEAD_DIM_A, HEAD_DIM_A), F32)).astype(BF16)
    zero_state = jnp.zeros((b, N_HEADS_R, KEY_DIM_R, VAL_DIM_R), F32)

    kp_l, vp_l, sp_l, ks_l, vs_l, ss_l = [], [], [], [], [], []
    for l in range(depth):
        row = lambda a: a[l].reshape(1, -1)
        w1a, w3a, w2a = w1_ffn1[l].astype(BF16), w3_ffn1[l].astype(BF16), w2_ffn1[l].astype(BF16)
        w1b, w3b, w2b = w1_ffn2[l].astype(BF16), w3_ffn2[l].astype(BF16), w2_ffn2[l].astype(BF16)
        win = w_in[l].astype(BF16)
        wpa, wpr, wo = w_pa[l].astype(BF16), w_pr[l].astype(BF16), w_o[l].astype(BF16)
        gq = jnp.tile(g_q[l], N_HEADS_A).reshape(1, W_A)
        gk = jnp.tile(g_k[l], N_HEADS_A).reshape(1, W_A)
        gain, bias = row(gn_gain), row(gn_bias)

        xp = _ffn(xp, row(g_ffn1), w1a, w3a, w2a, tm_p)
        q, kf, kb, vf, vb, qr, kr, vr, gr, ga, gb = _inproj(xp, row(g_mix), win, gq, gk, head_ones, tm_p)
        r3 = lambda a: a.reshape(b, s, a.shape[-1])
        oa = _moba_prompt(r3(q), r3(kb), r3(vb), r3(kf), col_bias)
        o_r, st = _retention(lg, r3(qr), r3(kr), r3(vr), r3(gr), gain, bias, zero_state,
                             RET_CHUNK_PROMPT, RET_CHUNK_PROMPT)
        xp = _outproj(oa.reshape(mp, W_A), o_r.reshape(mp, W_R), ga, gb, xp, wpa, wpr, wo, tm_p)
        xp = _ffn(xp, row(g_ffn2), w1b, w3b, w2b, tm_p)
        kp_l.append(kf.reshape(b, s, N_HEADS_A, HEAD_DIM_A))
        vp_l.append(vf.reshape(b, s, N_HEADS_A, HEAD_DIM_A))
        sp_l.append(st)

        xs = _ffn(xs, row(g_ffn1), w1a, w3a, w2a, tm_s)
        q, kf, kb, vf, vb, qr, kr, vr, gr, ga, gb = _inproj(xs, row(g_mix), win, gq, gk, head_ones, tm_s)
        r3 = lambda a: a.reshape(bd, SAMPLE_ROWS, a.shape[-1])
        oa = _moba_sample(page_table, r3(q), r3(kb), r3(vb), cache_kt, cache_vt, l, sd)
        o_r, st = _retention(lg, r3(qr), r3(kr), r3(vr), r3(gr), gain, bias, state_ret[l],
                             SAMPLE_ROWS, sd)
        xs = _outproj(oa.reshape(ms, W_A), o_r.reshape(ms, W_R), ga, gb, xs, wpa, wpr, wo, tm_s)
        xs = _ffn(xs, row(g_ffn2), w1b, w3b, w2b, tm_s)
        ks_l.append(r3(kf)[:, :sd].reshape(bd, sd, N_HEADS_A, HEAD_DIM_A))
        vs_l.append(r3(vf)[:, :sd].reshape(bd, sd, N_HEADS_A, HEAD_DIM_A))
        ss_l.append(st)

    y_prompt = xp.reshape(b, s, D_MODEL)
    y_sample = xs.reshape(bd, SAMPLE_ROWS, D_MODEL)[:, :sd]
    return (y_prompt, y_sample, jnp.stack(kp_l), jnp.stack(vp_l), jnp.stack(sp_l),
            jnp.stack(ks_l), jnp.stack(vs_l), jnp.stack(ss_l))
```

```python
import functools

import jax
import jax.numpy as jnp
from jax import lax
from jax.experimental import pallas as pl
from jax.experimental.pallas import tpu as pltpu

F32 = jnp.float32
BF16 = jnp.bfloat16

D_MODEL = 1024
D_FF = 2816
N_HEADS_A = 8
HEAD_DIM_A = 64
MOBA_BLOCK = 256
MOBA_TOPK = 3
N_HEADS_R = 4
KEY_DIM_R = 128
VAL_DIM_R = 128
EPS = 1e-6
NEG_INF = -1e30
W_A = N_HEADS_A * HEAD_DIM_A
W_R = N_HEADS_R * KEY_DIM_R

LANES = 128
VMEM_LIMIT_BYTES = 56 * 1024 * 1024
SAMPLE_ROWS = 16
RET_CHUNK_PROMPT = 256
RET_HEADS_PROMPT = 2
PAGES_PER_STEP = 32


def _nt(a, b):
    return lax.dot_general(a, b, (((1,), (1,)), ((), ())), preferred_element_type=F32)


def _tn(a, b):
    return lax.dot_general(a, b, (((0,), (0,)), ((), ())), preferred_element_type=F32)


def _mm(a, b):
    return jnp.dot(a, b, preferred_element_type=F32)


def _rms(x, g):
    return x * lax.rsqrt(jnp.mean(x * x, axis=-1, keepdims=True) + EPS) * g


def _split_bf16(x):
    hi = x.astype(BF16)
    lo = (x - hi.astype(F32)).astype(BF16)
    return hi, lo


def _params(*sem):
    return pltpu.CompilerParams(dimension_semantics=sem, vmem_limit_bytes=VMEM_LIMIT_BYTES)


def _row_spec(tm, cols):
    return pl.BlockSpec((tm, cols), lambda i: (i, 0))


def _const_spec(shape):
    return pl.BlockSpec(shape, lambda i: (0,) * len(shape), pipeline_mode=pl.Buffered(1))


def _layer_spec(layer, shape):
    return pl.BlockSpec((None,) + shape, lambda i: (layer, 0, 0), pipeline_mode=pl.Buffered(1))


def _ffn_body(x_ref, g_ref, w1_ref, w3_ref, w2_ref, o_ref):
    x = x_ref[...]
    h = _rms(x, g_ref[...]).astype(BF16)
    a = _mm(h, w1_ref[...])
    b = _mm(h, w3_ref[...])
    act = (a * jax.nn.sigmoid(a) * b).astype(BF16)
    o_ref[...] = x + 0.5 * _mm(act, w2_ref[...])


def _ffn(x, layer, g, w1, w3, w2, tm):
    m = x.shape[0]
    return pl.pallas_call(
        _ffn_body,
        grid=(m // tm,),
        in_specs=[_row_spec(tm, D_MODEL), _layer_spec(layer, (1, D_MODEL)),
                  _layer_spec(layer, (D_MODEL, D_FF)), _layer_spec(layer, (D_MODEL, D_FF)),
                  _layer_spec(layer, (D_FF, D_MODEL))],
        out_specs=_row_spec(tm, D_MODEL),
        out_shape=jax.ShapeDtypeStruct((m, D_MODEL), F32),
        compiler_params=_params("parallel"),
        name="ffn",
    )(x, g, w1, w3, w2)


_IN_COLS = (W_A, W_A, W_A, W_R, W_R, W_R, W_R, D_MODEL, D_MODEL)
_IN_OFFS = tuple(sum(_IN_COLS[:i]) for i in range(len(_IN_COLS) + 1))
D_IN = _IN_OFFS[-1]


def _inproj_body(x_ref, g_ref, w_ref, gq_ref, gk_ref, e_ref, *rest):
    q_o, kf_o, kb_o, vf_o, vb_o, qr_o, kr_o, vr_o, gr_o, ga_o, gb_o = rest[-11:]
    h = _rms(x_ref[...], g_ref[...]).astype(BF16)

    def proj(i):
        return _mm(h, w_ref[:, _IN_OFFS[i]:_IN_OFFS[i + 1]])

    def head_rms(z, gain):
        ss = _mm((z * z).astype(BF16), e_ref[...])
        return z * lax.rsqrt(ss * (1.0 / HEAD_DIM_A) + EPS) * gain

    q_o[...] = head_rms(proj(0), gq_ref[...]) * (HEAD_DIM_A ** -0.5)
    k = head_rms(proj(1), gk_ref[...])
    kf_o[...] = k
    kb_o[...] = k.astype(BF16)
    v = proj(2)
    vf_o[...] = v
    vb_o[...] = v.astype(BF16)
    qr_o[...] = proj(3).astype(BF16)
    kr_o[...] = (proj(4) * (KEY_DIM_R ** -0.5)).astype(BF16)
    vr_o[...] = proj(5).astype(BF16)
    gr_o[...] = proj(6)
    ga_o[...] = proj(7)
    gb_o[...] = proj(8)


def _inproj(x, layer, depth, g, w, gq, gk, e, tm, kv_stacks=None):
    m = x.shape[0]
    outs = [(W_A, F32), (W_A, F32), (W_A, BF16), (W_A, F32), (W_A, BF16),
            (W_R, BF16), (W_R, BF16), (W_R, BF16), (W_R, F32), (D_MODEL, F32), (D_MODEL, F32)]
    stacked = (1, 3)
    out_specs = [pl.BlockSpec((None, tm, c), lambda i: (layer, i, 0)) if n in stacked
                 else _row_spec(tm, c) for n, (c, _) in enumerate(outs)]
    out_shape = [jax.ShapeDtypeStruct((depth, m, c) if n in stacked else (m, c), dt)
                 for n, (c, dt) in enumerate(outs)]
    in_specs = [_row_spec(tm, D_MODEL), _layer_spec(layer, (1, D_MODEL)),
                _layer_spec(layer, (D_MODEL, D_IN)), _layer_spec(layer, (1, W_A)),
                _layer_spec(layer, (1, W_A)), _const_spec((W_A, W_A))]
    args = [x, g, w, gq, gk, e]
    aliases = {}
    if kv_stacks is not None:
        for n, stack in zip(stacked, kv_stacks):
            aliases[len(args)] = n
            in_specs.append(pl.BlockSpec(memory_space=pl.ANY))
            args.append(stack)
    return pl.pallas_call(
        _inproj_body,
        grid=(m // tm,),
        in_specs=in_specs,
        out_specs=out_specs,
        out_shape=out_shape,
        input_output_aliases=aliases,
        compiler_params=_params("parallel"),
        name="inproj",
    )(*args)


def _outproj_body(oa_ref, or_ref, ga_ref, gb_ref, x_ref, wpa_ref, wpr_ref, wo_ref, o_ref):
    a = _mm(oa_ref[...], wpa_ref[...])
    r = _mm(or_ref[...], wpr_ref[...])
    merged = jax.nn.sigmoid(ga_ref[...]) * a + jax.nn.sigmoid(gb_ref[...]) * r
    o_ref[...] = x_ref[...] + _mm(merged.astype(BF16), wo_ref[...])


def _outproj(oa, o_r, ga, gb, x, layer, wpa, wpr, wo, tm):
    m = x.shape[0]
    return pl.pallas_call(
        _outproj_body,
        grid=(m // tm,),
        in_specs=[_row_spec(tm, W_A), _row_spec(tm, W_R), _row_spec(tm, D_MODEL),
                  _row_spec(tm, D_MODEL), _row_spec(tm, D_MODEL),
                  _layer_spec(layer, (W_A, D_MODEL)), _layer_spec(layer, (W_R, D_MODEL)),
                  _layer_spec(layer, (D_MODEL, D_MODEL))],
        out_specs=_row_spec(tm, D_MODEL),
        out_shape=jax.ShapeDtypeStruct((m, D_MODEL), F32),
        compiler_params=_params("parallel"),
        name="outproj",
    )(oa, o_r, ga, gb, x, wpa, wpr, wo)


def _rank_select(gate, lane, n_valid_mask, candidates):
    gate = jnp.where(n_valid_mask, gate, NEG_INF)
    cnt = jnp.zeros(gate.shape, F32)
    for c in candidates:
        gc = gate[:, c:c + 1]
        beats = (gc > gate) | ((gc == gate) & (lane > c))
        cnt = cnt + jnp.where(beats, 1.0, 0.0)
    return jnp.where((cnt < MOBA_TOPK) & n_valid_mask, 1.0, 0.0)


def _moba_tables(s):
    nb = s // MOBA_BLOCK
    pos = jnp.arange(s)
    blk = (pos // MOBA_BLOCK).astype(F32)
    within = (pos % MOBA_BLOCK).astype(F32)
    ind = (pos[:, None] // MOBA_BLOCK == jnp.arange(nb)[None, :]).astype(F32)
    slopes = 2.0 ** (-8.0 * (jnp.arange(N_HEADS_A, dtype=F32) + 1.0) / N_HEADS_A)

    def half(slope):
        cols = [ind, (slope * MOBA_BLOCK * blk)[:, None], (slope * within)[:, None],
                jnp.zeros((s, HEAD_DIM_A - nb - 2), F32)]
        return jnp.concatenate(cols, axis=1)

    pairs = [jnp.concatenate([half(slopes[2 * p + 1]), half(slopes[2 * p])], axis=1)
             for p in range(N_HEADS_A // 2)]
    return jnp.stack(pairs).astype(BF16)


def _moba_prompt_body(q_ref, k_ref, v_ref, k32_ref, tab_ref, o_ref, qa_ref, qb_ref, ka_ref, kb_ref,
                      *, nb):
    blk = MOBA_BLOCK
    s_len = nb * blk
    q2 = q_ref[0]
    lane = lax.broadcasted_iota(jnp.int32, (1, LANES), 1)
    lo_half = lane < HEAD_DIM_A

    km = jnp.mean(k32_ref[0].reshape(nb, blk, LANES), axis=1)
    km_parts = []
    for part in (jnp.where(lo_half, km, 0.0), jnp.where(lo_half, 0.0, km)):
        km_parts.extend(_split_bf16(part))
    kms = jnp.concatenate(km_parts, axis=0)
    q_hi, q_lo = _split_bf16(q2)
    r_hi = _nt(kms, q_hi)
    r_lo = _nt(kms, q_lo)
    gates = [r_hi[0:nb] + r_hi[nb:2 * nb] + r_lo[0:nb],
             r_hi[2 * nb:3 * nb] + r_hi[3 * nb:4 * nb] + r_lo[2 * nb:3 * nb]]

    brow = lax.broadcasted_iota(jnp.int32, (nb, s_len), 0)
    own = lax.broadcasted_iota(jnp.int32, (nb, s_len), 1) // blk
    past = brow < own

    def block_bias(gate):
        g = jnp.where(past, gate, NEG_INF)
        cnt = jnp.zeros((nb, s_len), F32)
        for c in range(nb):
            gc = g[c:c + 1, :]
            beats = (gc > g) | ((gc == g) & (brow > c))
            cnt = cnt + jnp.where(beats, 1.0, 0.0)
        allowed = ((cnt < MOBA_TOPK) & past) | (brow == own)
        return jnp.where(allowed, 0.0, NEG_INF)

    ones2 = jnp.where(lax.broadcasted_iota(jnp.int32, (8, s_len), 0) < 2, 1.0, 0.0)
    pad = jnp.zeros((HEAD_DIM_A - nb - 8, s_len), F32)
    ext_t = jnp.concatenate([block_bias(gates[1]), ones2, pad, block_bias(gates[0]), ones2, pad], axis=0)
    ext = ext_t.T
    qa_ref[...] = jnp.where(lo_half, q2, ext).astype(BF16)
    qb_ref[...] = jnp.where(lo_half, ext, q2).astype(BF16)
    k2 = k_ref[0]
    tab = tab_ref[0]
    ka_ref[...] = jnp.where(lo_half, k2, tab)
    kb_ref[...] = jnp.where(lo_half, tab, k2)

    row = lax.broadcasted_iota(jnp.int32, (blk, blk), 0)
    col = lax.broadcasted_iota(jnp.int32, (blk, blk), 1)
    causal = col <= row

    for j in range(nb):
        lo, hi = j * blk, (j + 1) * blk
        outs = []
        for qx_ref, kx_ref in ((qa_ref, ka_ref), (qb_ref, kb_ref)):
            qx = qx_ref[lo:hi, :]
            s_own = jnp.where(causal, _nt(qx, kx_ref[lo:hi, :]), NEG_INF)
            m = jnp.max(s_own, axis=-1, keepdims=True)
            if j > 0:
                s_past = _nt(qx, kx_ref[0:lo, :])
                m = jnp.maximum(m, jnp.max(s_past, axis=-1, keepdims=True))
            p_own = jnp.exp(s_own - m)
            l = jnp.sum(p_own, axis=-1, keepdims=True)
            acc = _mm(p_own.astype(BF16), v_ref[0, lo:hi, :])
            if j > 0:
                p_past = jnp.exp(s_past - m)
                l = l + jnp.sum(p_past, axis=-1, keepdims=True)
                acc = acc + _mm(p_past.astype(BF16), v_ref[0, 0:lo, :])
            outs.append(acc / l)
        o_ref[0, lo:hi, :] = jnp.where(lo_half, outs[0], outs[1]).astype(o_ref.dtype)


def _moba_prompt(q, kb, vb, kf, kf_offset, tables):
    b, s, _ = q.shape
    nb = s // MOBA_BLOCK
    assert nb == 8, "block-indicator columns are laid out for 8 MoBA blocks"
    npair = N_HEADS_A // 2
    seq_spec = pl.BlockSpec((1, s, LANES), lambda bi, p: (bi, 0, p))
    return pl.pallas_call(
        functools.partial(_moba_prompt_body, nb=nb),
        grid=(b, npair),
        in_specs=[seq_spec, seq_spec, seq_spec,
                  pl.BlockSpec((1, s, LANES), lambda bi, p: (kf_offset + bi, 0, p)),
                  pl.BlockSpec((1, s, LANES), lambda bi, p: (p, 0, 0))],
        out_specs=seq_spec,
        out_shape=jax.ShapeDtypeStruct((b, s, W_A), BF16),
        scratch_shapes=[pltpu.VMEM((s, LANES), BF16)] * 4,
        compiler_params=_params("parallel", "parallel"),
        name="moba_prompt",
    )(q, kb, vb, kf, tables)


def _moba_sample_body(pt_ref, q_ref, kn_ref, vn_ref, *rest, n_real, past_len):
    npg = PAGES_PER_STEP
    kp = rest[0:npg]
    vp = rest[npg:2 * npg]
    o_ref = rest[2 * npg]
    s_ref, p_ref, g_ref, acc_ref, l_ref = rest[2 * npg + 1:]
    ph = pl.program_id(1)
    g = pl.program_id(2)
    n_groups = pl.num_programs(2)
    n_pages = s_ref.shape[0]
    rows = N_HEADS_A * n_real

    rowi = lax.broadcasted_iota(jnp.int32, (rows, W_A), 0)
    lanei = lax.broadcasted_iota(jnp.int32, (rows, W_A), 1)
    head_mask = (lanei >> 6) == (rowi & (N_HEADS_A - 1))
    q = q_ref[0]
    q_rep = jnp.concatenate(
        [jnp.broadcast_to(q[t:t + 1, :], (N_HEADS_A, W_A)) for t in range(n_real)], axis=0)
    q_hi, q_lo = _split_bf16(jnp.where(head_mask, q_rep, 0.0))
    q2 = jnp.concatenate([q_hi, q_lo], axis=0)

    def raw_scores(keys_t=None, keys=None):
        s2 = _mm(q2, keys_t) if keys_t is not None else _nt(q2, keys)
        return s2[:rows] + s2[rows:]

    lane = lax.broadcasted_iota(jnp.int32, (1, LANES), 1)

    @pl.when(ph == 0)
    def _():
        @pl.when(g == 0)
        def _():
            g_ref[...] = jnp.zeros_like(g_ref)

        gsum = g_ref[...]
        for n in range(npg):
            pg = g * npg + n
            s = raw_scores(keys_t=kp[n][0, 0].astype(BF16))
            s_ref[pg] = s
            gsum = jnp.where(lane == pg, jnp.sum(s, axis=-1, keepdims=True), gsum)
        g_ref[...] = gsum

    @pl.when((ph == 1) & (g == 0))
    def _():
        gsum = g_ref[...]
        gate = (gsum + pltpu.roll(gsum, LANES - 1, 1)) * (1.0 / MOBA_BLOCK)
        valid = ((lane & 1) == 0) & (lane < n_pages)
        sel = _rank_select(gate, lane, valid, range(0, n_pages, 2))

        rcol = lax.broadcasted_iota(jnp.int32, (rows, 1), 0)
        slope = jnp.exp2(-((rcol & (N_HEADS_A - 1)) + 1).astype(F32))
        qpos = past_len + (rcol >> 3)

        m = jnp.full((rows, 1), NEG_INF, F32)
        for pg in range(n_pages):
            c = 2 * (pg // 2)
            kpos = pg * LANES + lane
            s = s_ref[pg] - slope * (qpos - kpos).astype(F32)
            s = jnp.where(sel[:, c:c + 1] > 0.5, s, NEG_INF)
            s_ref[pg] = s
            m = jnp.maximum(m, jnp.max(s, axis=-1, keepdims=True))
        tnew = lax.broadcasted_iota(jnp.int32, (1, SAMPLE_ROWS), 1)
        dist = (rcol >> 3) - tnew
        s_own = raw_scores(keys=kn_ref[0]) - slope * dist.astype(F32)
        s_own = jnp.where(dist >= 0, s_own, NEG_INF)
        m = jnp.maximum(m, jnp.max(s_own, axis=-1, keepdims=True))

        l = jnp.zeros((rows, 1), F32)
        for pg in range(n_pages):
            p = jnp.exp(s_ref[pg] - m)
            l = l + jnp.sum(p, axis=-1, keepdims=True)
            p_ref[pg] = p.astype(BF16)
        p_own = jnp.exp(s_own - m)
        l = l + jnp.sum(p_own, axis=-1, keepdims=True)
        acc_ref[...] = _mm(p_own.astype(BF16), vn_ref[0])
        l_ref[...] = jnp.broadcast_to(l, l_ref.shape)

    @pl.when(ph == 1)
    def _():
        acc = acc_ref[...]
        for n in range(npg):
            acc = acc + _nt(p_ref[g * npg + n], vp[n][0, 0].astype(BF16))
        acc_ref[...] = acc

    @pl.when((ph == 1) & (g == n_groups - 1))
    def _():
        o = jnp.where(head_mask, acc_ref[...] / l_ref[:, 0:1], 0.0)
        out_rows = [jnp.sum(o[N_HEADS_A * t:N_HEADS_A * (t + 1), :], axis=0, keepdims=True)
                    for t in range(n_real)]
        out_rows.append(jnp.zeros((SAMPLE_ROWS - n_real, W_A), F32))
        o_ref[0] = jnp.concatenate(out_rows, axis=0).astype(o_ref.dtype)


def _moba_sample(page_table, q, kn, vn, cache_kt, cache_vt, layer, n_real):
    bd, n_pages = page_table.shape
    page = cache_kt.shape[-1]
    assert page == LANES and MOBA_BLOCK == 2 * page and n_pages % PAGES_PER_STEP == 0
    npg = PAGES_PER_STEP
    n_groups = n_pages // npg
    rows = N_HEADS_A * n_real

    def page_spec(n, is_k):
        def index_map(b, ph, g, pt):
            grp = jnp.where(ph == 0, g, n_groups - 1) if is_k else jnp.where(ph == 0, 0, g)
            return (layer, pt[b, grp * npg + n], 0, 0)
        return pl.BlockSpec((1, 1, W_A, page), index_map)

    tok_spec = pl.BlockSpec((1, SAMPLE_ROWS, W_A), lambda b, ph, g, pt: (b, 0, 0))
    grid_spec = pltpu.PrefetchScalarGridSpec(
        num_scalar_prefetch=1,
        grid=(bd, 2, n_groups),
        in_specs=[tok_spec, tok_spec, tok_spec]
        + [page_spec(n, True) for n in range(npg)] + [page_spec(n, False) for n in range(npg)],
        out_specs=tok_spec,
        scratch_shapes=[pltpu.VMEM((n_pages, rows, LANES), F32),
                        pltpu.VMEM((n_pages, rows, LANES), BF16),
                        pltpu.VMEM((rows, LANES), F32),
                        pltpu.VMEM((rows, W_A), F32),
                        pltpu.VMEM((rows, LANES), F32)],
    )
    return pl.pallas_call(
        functools.partial(_moba_sample_body, n_real=n_real, past_len=n_pages * page),
        grid_spec=grid_spec,
        out_shape=jax.ShapeDtypeStruct((bd, SAMPLE_ROWS, W_A), BF16),
        compiler_params=_params("parallel", "arbitrary", "arbitrary"),
        name="moba_sample",
    )(page_table, q, kn, vn, *([cache_kt] * npg), *([cache_vt] * npg))


def _log_decay_rows():
    log_decay = jnp.log(1.0 - 2.0 ** (-5.0 - jnp.arange(N_HEADS_R, dtype=F32)))
    return jnp.broadcast_to(log_decay.reshape(N_HEADS_R, 1, 1), (N_HEADS_R, 1, LANES))


def _retention_body(lg_ref, q_ref, k_ref, v_ref, gr_ref, gain_ref, bias_ref, s0_ref,
                    o_ref, s_out_ref, *, chunk, n_real, n_chunks, heads):
    hd = KEY_DIM_R
    ii = lax.broadcasted_iota(jnp.int32, (chunk, chunk), 0)
    jj = lax.broadcasted_iota(jnp.int32, (chunk, chunk), 1)
    diff = (ii - jj).astype(F32)
    icol = lax.broadcasted_iota(jnp.int32, (chunk, 1), 0).astype(F32)
    for h in range(heads):
        cols = slice(h * hd, (h + 1) * hd)
        lg = lg_ref[h][:, 0:1]
        dmat = jnp.where(diff >= 0, jnp.exp(lg * jnp.maximum(diff, 0.0)), 0.0)
        qdec = jnp.exp(lg * (icol + 1.0))
        kdec = jnp.where(icol < n_real, jnp.exp(lg * (n_real - 1.0 - icol)), 0.0)
        chunk_decay = jnp.exp(lg * float(n_real))
        gain = gain_ref[:, cols]
        bias = bias_ref[:, cols]
        state = s0_ref[0, h]
        for c in range(n_chunks):
            rows = slice(c * chunk, (c + 1) * chunk)
            qc = q_ref[0, rows, cols]
            kc = k_ref[0, rows, cols]
            vc = v_ref[0, rows, cols]
            inner = _nt(qc, kc) * dmat
            o = _mm(inner.astype(BF16), vc) + _mm(qc, state.astype(BF16)) * qdec
            kd = (kc.astype(F32) * kdec).astype(BF16)
            state = chunk_decay * state + _tn(kd, vc)
            mu = jnp.mean(o, axis=-1, keepdims=True)
            d = o - mu
            var = jnp.mean(d * d, axis=-1, keepdims=True)
            y = d * lax.rsqrt(var + EPS) * gain + bias
            gr = gr_ref[0, rows, cols]
            o_ref[0, rows, cols] = (y * (gr * jax.nn.sigmoid(gr))).astype(o_ref.dtype)
        s_out_ref[0, h] = state


def _retention(lg, q, k, v, gr, gain, bias, layer, state0, state_offset, chunk, n_real, heads):
    b, s, _ = q.shape
    hd = KEY_DIM_R
    width = heads * hd
    seq_spec = pl.BlockSpec((1, s, width), lambda bi, g: (bi, 0, g))
    vec_spec = pl.BlockSpec((None, 1, width), lambda bi, g: (layer, 0, g))
    st_spec = pl.BlockSpec((1, heads, hd, hd), lambda bi, g: (bi, g, 0, 0))
    st_in_spec = pl.BlockSpec((1, heads, hd, hd), lambda bi, g: (state_offset + bi, g, 0, 0))
    return pl.pallas_call(
        functools.partial(_retention_body, chunk=chunk, n_real=n_real, n_chunks=s // chunk,
                          heads=heads),
        grid=(b, N_HEADS_R // heads),
        in_specs=[pl.BlockSpec((heads, 1, LANES), lambda bi, g: (g, 0, 0)),
                  seq_spec, seq_spec, seq_spec, seq_spec, vec_spec, vec_spec, st_in_spec],
        out_specs=[seq_spec, st_spec],
        out_shape=[jax.ShapeDtypeStruct((b, s, W_R), BF16),
                   jax.ShapeDtypeStruct((b, N_HEADS_R, hd, hd), F32)],
        compiler_params=_params("parallel", "parallel"),
        name="retention",
    )(lg, q, k, v, gr, gain, bias, state0)


def kernel(x_prompt, x_sample, cache_k, cache_v, state_ret, page_table, g_ffn1, w1_ffn1, w3_ffn1,
           w2_ffn1, g_mix, w_in, g_q, g_k, gn_gain, gn_bias, w_pa, w_pr, w_o, g_ffn2, w1_ffn2,
           w3_ffn2, w2_ffn2):
    b, s, _ = x_prompt.shape
    bd, sd, _ = x_sample.shape
    depth = w_in.shape[0]
    n_pool, page = cache_k.shape[1], cache_k.shape[2]
    mp, ms = b * s, bd * SAMPLE_ROWS
    tm_p, tm_s = 512, ms

    xp = x_prompt.reshape(mp, D_MODEL)
    xs = jnp.pad(x_sample, ((0, 0), (0, SAMPLE_ROWS - sd), (0, 0))).reshape(ms, D_MODEL)

    cache_kt = jnp.transpose(cache_k, (0, 1, 3, 4, 2)).reshape(depth, n_pool, W_A, page)
    cache_vt = jnp.transpose(cache_v, (0, 1, 3, 4, 2)).reshape(depth, n_pool, W_A, page)

    tables = _moba_tables(s)
    lg = _log_decay_rows()
    head_ones = jnp.kron(jnp.eye(N_HEADS_A, dtype=F32),
                         jnp.ones((HEAD_DIM_A, HEAD_DIM_A), F32)).astype(BF16)
    zero_state = jnp.zeros((b, N_HEADS_R, KEY_DIM_R, VAL_DIM_R), F32)
    states_in = state_ret.reshape(depth * bd, N_HEADS_R, KEY_DIM_R, VAL_DIM_R)

    vec = lambda a: a.reshape(depth, 1, -1)
    g1, gm, g2, gain, bias = vec(g_ffn1), vec(g_mix), vec(g_ffn2), vec(gn_gain), vec(gn_bias)
    gq, gk = vec(jnp.tile(g_q, (1, N_HEADS_A))), vec(jnp.tile(g_k, (1, N_HEADS_A)))
    w1a, w3a, w2a = w1_ffn1.astype(BF16), w3_ffn1.astype(BF16), w2_ffn1.astype(BF16)
    w1b, w3b, w2b = w1_ffn2.astype(BF16), w3_ffn2.astype(BF16), w2_ffn2.astype(BF16)
    win, wpa, wpr, wo = w_in.astype(BF16), w_pa.astype(BF16), w_pr.astype(BF16), w_o.astype(BF16)

    kv_p = kv_s = None
    sp_l, ss_l = [], []
    for l in range(depth):
        xp = _ffn(xp, l, g1, w1a, w3a, w2a, tm_p)
        q, kf, kb, vf, vb, qr, kr, vr, gr, ga, gb = _inproj(
            xp, l, depth, gm, win, gq, gk, head_ones, tm_p, kv_p)
        kv_p = (kf, vf)
        r3 = lambda a: a.reshape(-1, s, a.shape[-1])
        oa = _moba_prompt(r3(q), r3(kb), r3(vb), r3(kf), l * b, tables)
        o_r, st = _retention(lg, r3(qr), r3(kr), r3(vr), r3(gr), gain, bias, l, zero_state, 0,
                             RET_CHUNK_PROMPT, RET_CHUNK_PROMPT, RET_HEADS_PROMPT)
        xp = _outproj(oa.reshape(mp, W_A), o_r.reshape(mp, W_R), ga, gb, xp, l, wpa, wpr, wo, tm_p)
        xp = _ffn(xp, l, g2, w1b, w3b, w2b, tm_p)
        sp_l.append(st)

        xs = _ffn(xs, l, g1, w1a, w3a, w2a, tm_s)
        q, kf, kb, vf, vb, qr, kr, vr, gr, ga, gb = _inproj(
            xs, l, depth, gm, win, gq, gk, head_ones, tm_s, kv_s)
        kv_s = (kf, vf)
        r3 = lambda a: a.reshape(bd, SAMPLE_ROWS, a.shape[-1])
        oa = _moba_sample(page_table, r3(q), r3(kb), r3(vb), cache_kt, cache_vt, l, sd)
        o_r, st = _retention(lg, r3(qr), r3(kr), r3(vr), r3(gr), gain, bias, l, states_in, l * bd,
                             SAMPLE_ROWS, sd, N_HEADS_R)
        xs = _outproj(oa.reshape(ms, W_A), o_r.reshape(ms, W_R), ga, gb, xs, l, wpa, wpr, wo, tm_s)
        xs = _ffn(xs, l, g2, w1b, w3b, w2b, tm_s)
        ss_l.append(st)

    y_prompt = xp.reshape(b, s, D_MODEL)
    y_sample = xs.reshape(bd, SAMPLE_ROWS, D_MODEL)[:, :sd]
    heads5 = lambda a, n: a.reshape(depth, -1, n, N_HEADS_A, HEAD_DIM_A)
    k_prompt, v_prompt = heads5(kv_p[0], s), heads5(kv_p[1], s)
    k_sample, v_sample = heads5(kv_s[0], SAMPLE_ROWS)[:, :, :sd], heads5(kv_s[1], SAMPLE_ROWS)[:, :, :sd]
    return (y_prompt, y_sample, k_prompt, v_prompt, jnp.stack(sp_l),
            k_sample, v_sample, jnp.stack(ss_l))
```

```python
import functools

import jax
import jax.numpy as jnp
from jax import lax
from jax.experimental import pallas as pl
from jax.experimental.pallas import tpu as pltpu

F32 = jnp.float32
BF16 = jnp.bfloat16

D_MODEL = 1024
D_FF = 2816
N_HEADS_A = 8
HEAD_DIM_A = 64
MOBA_BLOCK = 256
MOBA_TOPK = 3
N_HEADS_R = 4
KEY_DIM_R = 128
VAL_DIM_R = 128
EPS = 1e-6
NEG_INF = -1e30
W_A = N_HEADS_A * HEAD_DIM_A
W_R = N_HEADS_R * KEY_DIM_R

LANES = 128
VMEM_LIMIT_BYTES = 56 * 1024 * 1024
SAMPLE_ROWS = 16
RET_CHUNK_PROMPT = 256
RET_HEADS_PROMPT = 2
PAGE_SLOTS = 32


def _nt(a, b):
    return lax.dot_general(a, b, (((1,), (1,)), ((), ())), preferred_element_type=F32)


def _tn(a, b):
    return lax.dot_general(a, b, (((0,), (0,)), ((), ())), preferred_element_type=F32)


def _mm(a, b):
    return jnp.dot(a, b, preferred_element_type=F32)


def _rms(x, g):
    return x * lax.rsqrt(jnp.mean(x * x, axis=-1, keepdims=True) + EPS) * g


def _split_bf16(x):
    hi = x.astype(BF16)
    lo = (x - hi.astype(F32)).astype(BF16)
    return hi, lo


def _params(*sem):
    return pltpu.CompilerParams(dimension_semantics=sem, vmem_limit_bytes=VMEM_LIMIT_BYTES)


def _row_spec(tm, cols):
    return pl.BlockSpec((tm, cols), lambda i: (i, 0))


def _const_spec(shape):
    return pl.BlockSpec(shape, lambda i: (0,) * len(shape), pipeline_mode=pl.Buffered(1))


def _layer_spec(layer, shape):
    return pl.BlockSpec((None,) + shape, lambda i: (layer, 0, 0), pipeline_mode=pl.Buffered(1))


def _ffn_body(x_ref, g_ref, w1_ref, w3_ref, w2_ref, o_ref):
    x = x_ref[...]
    h = _rms(x, g_ref[...]).astype(BF16)
    a = _mm(h, w1_ref[...])
    b = _mm(h, w3_ref[...])
    act = (a * jax.nn.sigmoid(a) * b).astype(BF16)
    o_ref[...] = x + 0.5 * _mm(act, w2_ref[...])


def _ffn(x, layer, g, w1, w3, w2, tm):
    m = x.shape[0]
    return pl.pallas_call(
        _ffn_body,
        grid=(m // tm,),
        in_specs=[_row_spec(tm, D_MODEL), _layer_spec(layer, (1, D_MODEL)),
                  _layer_spec(layer, (D_MODEL, D_FF)), _layer_spec(layer, (D_MODEL, D_FF)),
                  _layer_spec(layer, (D_FF, D_MODEL))],
        out_specs=_row_spec(tm, D_MODEL),
        out_shape=jax.ShapeDtypeStruct((m, D_MODEL), F32),
        compiler_params=_params("parallel"),
        name="ffn",
    )(x, g, w1, w3, w2)


_IN_COLS = (W_A, W_A, W_A, W_R, W_R, W_R, W_R, D_MODEL, D_MODEL)
_IN_OFFS = tuple(sum(_IN_COLS[:i]) for i in range(len(_IN_COLS) + 1))
D_IN = _IN_OFFS[-1]


def _inproj_body(x_ref, g_ref, w_ref, gq_ref, gk_ref, e_ref, *rest, seq_major):
    n_out = 12 if seq_major else 11
    q_o, kf_o, kb_o, vf_o, vb_o, qr_o, kr_o, vr_o, gr_o, ga_o, gb_o = rest[-n_out:][:11]
    h = _rms(x_ref[...], g_ref[...]).astype(BF16)

    def proj(i):
        return _mm(h, w_ref[:, _IN_OFFS[i]:_IN_OFFS[i + 1]])

    def head_rms(z, gain):
        ss = _mm((z * z).astype(BF16), e_ref[...])
        return z * lax.rsqrt(ss * (1.0 / HEAD_DIM_A) + EPS) * gain

    q_o[...] = head_rms(proj(0), gq_ref[...]) * (HEAD_DIM_A ** -0.5)
    k = head_rms(proj(1), gk_ref[...])
    kb_o[...] = k.astype(BF16)
    v = proj(2)
    vb_o[...] = v.astype(BF16)
    if seq_major:
        kf_o[...] = k.T
        vf_o[...] = v.T
        km_o = rest[-1]
        for r in range(km_o.shape[0]):
            km_o[r] = jnp.mean(k[r * MOBA_BLOCK:(r + 1) * MOBA_BLOCK, :], axis=0, keepdims=True)
    else:
        kf_o[...] = k
        vf_o[...] = v
    qr_o[...] = proj(3).astype(BF16)
    kr_o[...] = (proj(4) * (KEY_DIM_R ** -0.5)).astype(BF16)
    vr_o[...] = proj(5).astype(BF16)
    gr_o[...] = proj(6)
    ga_o[...] = proj(7)
    gb_o[...] = proj(8)


def _inproj(x, layer, depth, g, w, gq, gk, e, tm, kv_stacks=None, seq_len=None):
    m = x.shape[0]
    outs = [(W_A, F32), (W_A, F32), (W_A, BF16), (W_A, F32), (W_A, BF16),
            (W_R, BF16), (W_R, BF16), (W_R, BF16), (W_R, F32), (D_MODEL, F32), (D_MODEL, F32)]
    stacked = (1, 3)
    out_specs = [_row_spec(tm, c) for c, _ in outs]
    out_shape = [jax.ShapeDtypeStruct((m, c), dt) for c, dt in outs]
    for n in stacked:
        if seq_len is None:
            out_specs[n] = pl.BlockSpec((None, tm, W_A), lambda i: (layer, i, 0))
            out_shape[n] = jax.ShapeDtypeStruct((depth, m, W_A), F32)
        else:
            tiles = seq_len // tm
            out_specs[n] = pl.BlockSpec((None, None, W_A, tm),
                                        lambda i: (layer, i // tiles, 0, i % tiles))
            out_shape[n] = jax.ShapeDtypeStruct((depth, m // seq_len, W_A, seq_len), F32)
    if seq_len is not None:
        out_specs.append(pl.BlockSpec((tm // MOBA_BLOCK, 1, W_A), lambda i: (i, 0, 0)))
        out_shape.append(jax.ShapeDtypeStruct((m // MOBA_BLOCK, 1, W_A), F32))
    in_specs = [_row_spec(tm, D_MODEL), _layer_spec(layer, (1, D_MODEL)),
                _layer_spec(layer, (D_MODEL, D_IN)), _layer_spec(layer, (1, W_A)),
                _layer_spec(layer, (1, W_A)), _const_spec((W_A, W_A))]
    args = [x, g, w, gq, gk, e]
    aliases = {}
    if kv_stacks is not None:
        for n, stack in zip(stacked, kv_stacks):
            aliases[len(args)] = n
            in_specs.append(pl.BlockSpec(memory_space=pl.ANY))
            args.append(stack)
    return pl.pallas_call(
        functools.partial(_inproj_body, seq_major=seq_len is not None),
        grid=(m // tm,),
        in_specs=in_specs,
        out_specs=out_specs,
        out_shape=out_shape,
        input_output_aliases=aliases,
        compiler_params=_params("parallel"),
        name="inproj",
    )(*args)


def _outproj_body(oa_ref, or_ref, ga_ref, gb_ref, x_ref, wpa_ref, wpr_ref, wo_ref, o_ref):
    a = _mm(oa_ref[...], wpa_ref[...])
    r = _mm(or_ref[...], wpr_ref[...])
    merged = jax.nn.sigmoid(ga_ref[...]) * a + jax.nn.sigmoid(gb_ref[...]) * r
    o_ref[...] = x_ref[...] + _mm(merged.astype(BF16), wo_ref[...])


def _outproj(oa, o_r, ga, gb, x, layer, wpa, wpr, wo, tm):
    m = x.shape[0]
    return pl.pallas_call(
        _outproj_body,
        grid=(m // tm,),
        in_specs=[_row_spec(tm, W_A), _row_spec(tm, W_R), _row_spec(tm, D_MODEL),
                  _row_spec(tm, D_MODEL), _row_spec(tm, D_MODEL),
                  _layer_spec(layer, (W_A, D_MODEL)), _layer_spec(layer, (W_R, D_MODEL)),
                  _layer_spec(layer, (D_MODEL, D_MODEL))],
        out_specs=_row_spec(tm, D_MODEL),
        out_shape=jax.ShapeDtypeStruct((m, D_MODEL), F32),
        compiler_params=_params("parallel"),
        name="outproj",
    )(oa, o_r, ga, gb, x, wpa, wpr, wo)


def _rank_select(gate, lane, n_valid_mask, candidates):
    gate = jnp.where(n_valid_mask, gate, NEG_INF)
    cnt = jnp.zeros(gate.shape, F32)
    for c in candidates:
        gc = gate[:, c:c + 1]
        beats = (gc > gate) | ((gc == gate) & (lane > c))
        cnt = cnt + jnp.where(beats, 1.0, 0.0)
    return jnp.where((cnt < MOBA_TOPK) & n_valid_mask, 1.0, 0.0)


def _moba_tables(s):
    nb = s // MOBA_BLOCK
    pos = jnp.arange(s)
    blk = (pos // MOBA_BLOCK).astype(F32)
    within = (pos % MOBA_BLOCK).astype(F32)
    ind = (pos[:, None] // MOBA_BLOCK == jnp.arange(nb)[None, :]).astype(F32)
    slopes = 2.0 ** (-8.0 * (jnp.arange(N_HEADS_A, dtype=F32) + 1.0) / N_HEADS_A)

    def half(slope):
        cols = [ind, (slope * MOBA_BLOCK * blk)[:, None], (slope * within)[:, None],
                jnp.zeros((s, HEAD_DIM_A - nb - 2), F32)]
        return jnp.concatenate(cols, axis=1)

    pairs = [jnp.concatenate([half(slopes[2 * p + 1]), half(slopes[2 * p])], axis=1)
             for p in range(N_HEADS_A // 2)]
    return jnp.stack(pairs).astype(BF16)


def _moba_prompt_body(q_ref, k_ref, v_ref, km_ref, tab_ref, o_ref, qa_ref, qb_ref, ka_ref, kb_ref,
                      *, nb):
    blk = MOBA_BLOCK
    s_len = nb * blk
    q2 = q_ref[0]
    lane = lax.broadcasted_iota(jnp.int32, (1, LANES), 1)
    lo_half = lane < HEAD_DIM_A

    km = km_ref[0]
    km_parts = []
    for part in (jnp.where(lo_half, km, 0.0), jnp.where(lo_half, 0.0, km)):
        km_parts.extend(_split_bf16(part))
    kms = jnp.concatenate(km_parts, axis=0)
    q_hi, q_lo = _split_bf16(q2)
    r_hi = _nt(kms, q_hi)
    r_lo = _nt(kms, q_lo)
    gates = [r_hi[0:nb] + r_hi[nb:2 * nb] + r_lo[0:nb],
             r_hi[2 * nb:3 * nb] + r_hi[3 * nb:4 * nb] + r_lo[2 * nb:3 * nb]]

    brow = lax.broadcasted_iota(jnp.int32, (nb, s_len), 0)
    own = lax.broadcasted_iota(jnp.int32, (nb, s_len), 1) // blk
    past = brow < own

    def block_bias(gate):
        g = jnp.where(past, gate, NEG_INF)
        cnt = jnp.zeros((nb, s_len), F32)
        for c in range(nb):
            gc = g[c:c + 1, :]
            beats = (gc > g) | ((gc == g) & (brow > c))
            cnt = cnt + jnp.where(beats, 1.0, 0.0)
        allowed = ((cnt < MOBA_TOPK) & past) | (brow == own)
        return jnp.where(allowed, 0.0, NEG_INF)

    ones2 = jnp.where(lax.broadcasted_iota(jnp.int32, (8, s_len), 0) < 2, 1.0, 0.0)
    pad = jnp.zeros((HEAD_DIM_A - nb - 8, s_len), F32)
    ext_t = jnp.concatenate([block_bias(gates[1]), ones2, pad, block_bias(gates[0]), ones2, pad], axis=0)
    ext = ext_t.T
    qa_ref[...] = jnp.where(lo_half, q2, ext).astype(BF16)
    qb_ref[...] = jnp.where(lo_half, ext, q2).astype(BF16)
    k2 = k_ref[0]
    tab = tab_ref[0]
    ka_ref[...] = jnp.where(lo_half, k2, tab)
    kb_ref[...] = jnp.where(lo_half, tab, k2)

    row = lax.broadcasted_iota(jnp.int32, (blk, blk), 0)
    col = lax.broadcasted_iota(jnp.int32, (blk, blk), 1)
    causal = col <= row

    for j in range(nb):
        lo, hi = j * blk, (j + 1) * blk
        outs = []
        for qx_ref, kx_ref in ((qa_ref, ka_ref), (qb_ref, kb_ref)):
            qx = qx_ref[lo:hi, :]
            s_own = jnp.where(causal, _nt(qx, kx_ref[lo:hi, :]), NEG_INF)
            m = jnp.max(s_own, axis=-1, keepdims=True)
            if j > 0:
                s_past = _nt(qx, kx_ref[0:lo, :])
                m = jnp.maximum(m, jnp.max(s_past, axis=-1, keepdims=True))
            p_own = jnp.exp(s_own - m)
            l = jnp.sum(p_own, axis=-1, keepdims=True)
            acc = _mm(p_own.astype(BF16), v_ref[0, lo:hi, :])
            if j > 0:
                p_past = jnp.exp(s_past - m)
                l = l + jnp.sum(p_past, axis=-1, keepdims=True)
                acc = acc + _mm(p_past.astype(BF16), v_ref[0, 0:lo, :])
            outs.append(acc / l)
        o_ref[0, lo:hi, :] = jnp.where(lo_half, outs[0], outs[1]).astype(o_ref.dtype)


def _moba_prompt(q, kb, vb, km, tables):
    b, s, _ = q.shape
    nb = s // MOBA_BLOCK
    assert nb == 8, "block-indicator columns are laid out for 8 MoBA blocks"
    npair = N_HEADS_A // 2
    seq_spec = pl.BlockSpec((1, s, LANES), lambda bi, p: (bi, 0, p))
    return pl.pallas_call(
        functools.partial(_moba_prompt_body, nb=nb),
        grid=(b, npair),
        in_specs=[seq_spec, seq_spec, seq_spec,
                  pl.BlockSpec((1, nb, LANES), lambda bi, p: (bi, 0, p)),
                  pl.BlockSpec((1, s, LANES), lambda bi, p: (p, 0, 0))],
        out_specs=seq_spec,
        out_shape=jax.ShapeDtypeStruct((b, s, W_A), BF16),
        scratch_shapes=[pltpu.VMEM((s, LANES), BF16)] * 4,
        compiler_params=_params("parallel", "parallel"),
        name="moba_prompt",
    )(q, kb, vb, km, tables)


def _moba_sample_body(pt_ref, q_ref, kn_ref, vn_ref, kt_hbm, vt_hbm, o_ref, buf, sem, s_ref, p_ref,
                      *, layer, n_real, past_len):
    b = pl.program_id(0)
    n_seq = pl.num_programs(0)
    n_pages = s_ref.shape[0]
    ring = PAGE_SLOTS
    rows = N_HEADS_A * n_real

    def page_copy(hbm, seq, pg):
        slot = pg % ring
        return pltpu.make_async_copy(hbm.at[layer, pt_ref[seq, pg]], buf.at[slot], sem.at[slot])

    @pl.when(b == 0)
    def _():
        for pg in range(ring):
            page_copy(kt_hbm, b, pg).start()

    rowi = lax.broadcasted_iota(jnp.int32, (rows, W_A), 0)
    lanei = lax.broadcasted_iota(jnp.int32, (rows, W_A), 1)
    head_mask = (lanei >> 6) == (rowi & (N_HEADS_A - 1))
    q = q_ref[0]
    q_rep = jnp.concatenate(
        [jnp.broadcast_to(q[t:t + 1, :], (N_HEADS_A, W_A)) for t in range(n_real)], axis=0)
    q_hi, q_lo = _split_bf16(jnp.where(head_mask, q_rep, 0.0))
    q2 = jnp.concatenate([q_hi, q_lo], axis=0)

    def raw_scores(keys_t=None, keys=None):
        s2 = _mm(q2, keys_t) if keys_t is not None else _nt(q2, keys)
        return s2[:rows] + s2[rows:]

    lane = lax.broadcasted_iota(jnp.int32, (1, LANES), 1)

    gsum = jnp.zeros((rows, LANES), F32)
    for pg in range(n_pages):
        page_copy(kt_hbm, b, pg).wait()
        s = raw_scores(keys_t=buf[pg % ring].astype(BF16))
        s_ref[pg] = s
        gsum = jnp.where(lane == pg, jnp.sum(s, axis=-1, keepdims=True), gsum)
        if pg + ring < n_pages:
            page_copy(kt_hbm, b, pg + ring).start()
        else:
            page_copy(vt_hbm, b, pg + ring - n_pages).start()

    gate = (gsum + pltpu.roll(gsum, LANES - 1, 1)) * (1.0 / MOBA_BLOCK)
    valid = ((lane & 1) == 0) & (lane < n_pages)
    sel = _rank_select(gate, lane, valid, range(0, n_pages, 2))

    rcol = lax.broadcasted_iota(jnp.int32, (rows, 1), 0)
    slope = jnp.exp2(-((rcol & (N_HEADS_A - 1)) + 1).astype(F32))
    qpos = past_len + (rcol >> 3)

    m = jnp.full((rows, 1), NEG_INF, F32)
    for pg in range(n_pages):
        c = 2 * (pg // 2)
        kpos = pg * LANES + lane
        s = s_ref[pg] - slope * (qpos - kpos).astype(F32)
        s = jnp.where(sel[:, c:c + 1] > 0.5, s, NEG_INF)
        s_ref[pg] = s
        m = jnp.maximum(m, jnp.max(s, axis=-1, keepdims=True))
    tnew = lax.broadcasted_iota(jnp.int32, (1, SAMPLE_ROWS), 1)
    dist = (rcol >> 3) - tnew
    s_own = raw_scores(keys=kn_ref[0]) - slope * dist.astype(F32)
    s_own = jnp.where(dist >= 0, s_own, NEG_INF)
    m = jnp.maximum(m, jnp.max(s_own, axis=-1, keepdims=True))

    l = jnp.zeros((rows, 1), F32)
    for pg in range(n_pages):
        p = jnp.exp(s_ref[pg] - m)
        l = l + jnp.sum(p, axis=-1, keepdims=True)
        p_ref[pg] = p.astype(BF16)
    p_own = jnp.exp(s_own - m)
    l = l + jnp.sum(p_own, axis=-1, keepdims=True)
    acc = _mm(p_own.astype(BF16), vn_ref[0])

    for pg in range(n_pages):
        page_copy(vt_hbm, b, pg).wait()
        acc = acc + _nt(p_ref[pg], buf[pg % ring].astype(BF16))
        if pg + ring < n_pages:
            page_copy(vt_hbm, b, pg + ring).start()
        else:
            @pl.when(b + 1 < n_seq)
            def _():
                page_copy(kt_hbm, b + 1, pg + ring - n_pages).start()

    o = jnp.where(head_mask, acc / l, 0.0)
    out_rows = [jnp.sum(o[N_HEADS_A * t:N_HEADS_A * (t + 1), :], axis=0, keepdims=True)
                for t in range(n_real)]
    out_rows.append(jnp.zeros((SAMPLE_ROWS - n_real, W_A), F32))
    o_ref[0] = jnp.concatenate(out_rows, axis=0).astype(o_ref.dtype)


def _moba_sample(page_table, q, kn, vn, cache_kt, cache_vt, layer, n_real):
    bd, n_pages = page_table.shape
    page = cache_kt.shape[-1]
    assert page == LANES and MOBA_BLOCK == 2 * page
    assert PAGE_SLOTS <= n_pages and n_pages <= LANES
    rows = N_HEADS_A * n_real
    tok_spec = pl.BlockSpec((1, SAMPLE_ROWS, W_A), lambda b, pt: (b, 0, 0))
    hbm_spec = pl.BlockSpec(memory_space=pl.ANY)
    grid_spec = pltpu.PrefetchScalarGridSpec(
        num_scalar_prefetch=1,
        grid=(bd,),
        in_specs=[tok_spec, tok_spec, tok_spec, hbm_spec, hbm_spec],
        out_specs=tok_spec,
        scratch_shapes=[pltpu.VMEM((PAGE_SLOTS, W_A, page), F32),
                        pltpu.SemaphoreType.DMA((PAGE_SLOTS,)),
                        pltpu.VMEM((n_pages, rows, LANES), F32),
                        pltpu.VMEM((n_pages, rows, LANES), BF16)],
    )
    return pl.pallas_call(
        functools.partial(_moba_sample_body, layer=layer, n_real=n_real, past_len=n_pages * page),
        grid_spec=grid_spec,
        out_shape=jax.ShapeDtypeStruct((bd, SAMPLE_ROWS, W_A), BF16),
        compiler_params=_params("arbitrary"),
        name="moba_sample",
    )(page_table, q, kn, vn, cache_kt, cache_vt)


def _log_decay_rows():
    log_decay = jnp.log(1.0 - 2.0 ** (-5.0 - jnp.arange(N_HEADS_R, dtype=F32)))
    return jnp.broadcast_to(log_decay.reshape(N_HEADS_R, 1, 1), (N_HEADS_R, 1, LANES))


def _retention_body(lg_ref, q_ref, k_ref, v_ref, gr_ref, gain_ref, bias_ref, s0_ref,
                    o_ref, s_out_ref, *, chunk, n_real, n_chunks, heads):
    hd = KEY_DIM_R
    ii = lax.broadcasted_iota(jnp.int32, (chunk, chunk), 0)
    jj = lax.broadcasted_iota(jnp.int32, (chunk, chunk), 1)
    diff = (ii - jj).astype(F32)
    icol = lax.broadcasted_iota(jnp.int32, (chunk, 1), 0).astype(F32)
    for h in range(heads):
        cols = slice(h * hd, (h + 1) * hd)
        lg = lg_ref[h][:, 0:1]
        dmat = jnp.where(diff >= 0, jnp.exp(lg * jnp.maximum(diff, 0.0)), 0.0)
        qdec = jnp.exp(lg * (icol + 1.0))
        kdec = jnp.where(icol < n_real, jnp.exp(lg * (n_real - 1.0 - icol)), 0.0)
        chunk_decay = jnp.exp(lg * float(n_real))
        gain = gain_ref[:, cols]
        bias = bias_ref[:, cols]
        state = s0_ref[0, h]
        for c in range(n_chunks):
            rows = slice(c * chunk, (c + 1) * chunk)
            qc = q_ref[0, rows, cols]
            kc = k_ref[0, rows, cols]
            vc = v_ref[0, rows, cols]
            inner = _nt(qc, kc) * dmat
            o = _mm(inner.astype(BF16), vc) + _mm(qc, state.astype(BF16)) * qdec
            kd = (kc.astype(F32) * kdec).astype(BF16)
            state = chunk_decay * state + _tn(kd, vc)
            mu = jnp.mean(o, axis=-1, keepdims=True)
            d = o - mu
            var = jnp.mean(d * d, axis=-1, keepdims=True)
            y = d * lax.rsqrt(var + EPS) * gain + bias
            gr = gr_ref[0, rows, cols]
            o_ref[0, rows, cols] = (y * (gr * jax.nn.sigmoid(gr))).astype(o_ref.dtype)
        s_out_ref[0, h] = state


def _retention(lg, q, k, v, gr, gain, bias, layer, state0, state_offset, chunk, n_real, heads):
    b, s, _ = q.shape
    hd = KEY_DIM_R
    width = heads * hd
    seq_spec = pl.BlockSpec((1, s, width), lambda bi, g: (bi, 0, g))
    vec_spec = pl.BlockSpec((None, 1, width), lambda bi, g: (layer, 0, g))
    st_spec = pl.BlockSpec((1, heads, hd, hd), lambda bi, g: (bi, g, 0, 0))
    st_in_spec = pl.BlockSpec((1, heads, hd, hd), lambda bi, g: (state_offset + bi, g, 0, 0))
    return pl.pallas_call(
        functools.partial(_retention_body, chunk=chunk, n_real=n_real, n_chunks=s // chunk,
                          heads=heads),
        grid=(b, N_HEADS_R // heads),
        in_specs=[pl.BlockSpec((heads, 1, LANES), lambda bi, g: (g, 0, 0)),
                  seq_spec, seq_spec, seq_spec, seq_spec, vec_spec, vec_spec, st_in_spec],
        out_specs=[seq_spec, st_spec],
        out_shape=[jax.ShapeDtypeStruct((b, s, W_R), BF16),
                   jax.ShapeDtypeStruct((b, N_HEADS_R, hd, hd), F32)],
        compiler_params=_params("parallel", "parallel"),
        name="retention",
    )(lg, q, k, v, gr, gain, bias, state0)


def kernel(x_prompt, x_sample, cache_k, cache_v, state_ret, page_table, g_ffn1, w1_ffn1, w3_ffn1,
           w2_ffn1, g_mix, w_in, g_q, g_k, gn_gain, gn_bias, w_pa, w_pr, w_o, g_ffn2, w1_ffn2,
           w3_ffn2, w2_ffn2):
    b, s, _ = x_prompt.shape
    bd, sd, _ = x_sample.shape
    depth = w_in.shape[0]
    n_pool, page = cache_k.shape[1], cache_k.shape[2]
    mp, ms = b * s, bd * SAMPLE_ROWS
    tm_p, tm_s = 512, ms

    xp = x_prompt.reshape(mp, D_MODEL)
    xs = jnp.pad(x_sample, ((0, 0), (0, SAMPLE_ROWS - sd), (0, 0))).reshape(ms, D_MODEL)

    cache_kt = jnp.transpose(cache_k, (0, 1, 3, 4, 2)).reshape(depth, n_pool, W_A, page)
    cache_vt = jnp.transpose(cache_v, (0, 1, 3, 4, 2)).reshape(depth, n_pool, W_A, page)

    tables = _moba_tables(s)
    lg = _log_decay_rows()
    head_ones = jnp.kron(jnp.eye(N_HEADS_A, dtype=F32),
                         jnp.ones((HEAD_DIM_A, HEAD_DIM_A), F32)).astype(BF16)
    zero_state = jnp.zeros((b, N_HEADS_R, KEY_DIM_R, VAL_DIM_R), F32)
    states_in = state_ret.reshape(depth * bd, N_HEADS_R, KEY_DIM_R, VAL_DIM_R)

    vec = lambda a: a.reshape(depth, 1, -1)
    g1, gm, g2, gain, bias = vec(g_ffn1), vec(g_mix), vec(g_ffn2), vec(gn_gain), vec(gn_bias)
    gq, gk = vec(jnp.tile(g_q, (1, N_HEADS_A))), vec(jnp.tile(g_k, (1, N_HEADS_A)))
    w1a, w3a, w2a = w1_ffn1.astype(BF16), w3_ffn1.astype(BF16), w2_ffn1.astype(BF16)
    w1b, w3b, w2b = w1_ffn2.astype(BF16), w3_ffn2.astype(BF16), w2_ffn2.astype(BF16)
    win, wpa, wpr, wo = w_in.astype(BF16), w_pa.astype(BF16), w_pr.astype(BF16), w_o.astype(BF16)

    kv_p = kv_s = None
    sp_l, ss_l = [], []
    for l in range(depth):
        xp = _ffn(xp, l, g1, w1a, w3a, w2a, tm_p)
        q, kf, kb, vf, vb, qr, kr, vr, gr, ga, gb, km = _inproj(
            xp, l, depth, gm, win, gq, gk, head_ones, tm_p, kv_p, seq_len=s)
        kv_p = (kf, vf)
        r3 = lambda a: a.reshape(b, s, a.shape[-1])
        oa = _moba_prompt(r3(q), r3(kb), r3(vb), km.reshape(b, s // MOBA_BLOCK, W_A), tables)
        o_r, st = _retention(lg, r3(qr), r3(kr), r3(vr), r3(gr), gain, bias, l, zero_state, 0,
                             RET_CHUNK_PROMPT, RET_CHUNK_PROMPT, RET_HEADS_PROMPT)
        xp = _outproj(oa.reshape(mp, W_A), o_r.reshape(mp, W_R), ga, gb, xp, l, wpa, wpr, wo, tm_p)
        xp = _ffn(xp, l, g2, w1b, w3b, w2b, tm_p)
        sp_l.append(st)

        xs = _ffn(xs, l, g1, w1a, w3a, w2a, tm_s)
        q, kf, kb, vf, vb, qr, kr, vr, gr, ga, gb = _inproj(
            xs, l, depth, gm, win, gq, gk, head_ones, tm_s, kv_s)
        kv_s = (kf, vf)
        r3 = lambda a: a.reshape(bd, SAMPLE_ROWS, a.shape[-1])
        oa = _moba_sample(page_table, r3(q), r3(kb), r3(vb), cache_kt, cache_vt, l, sd)
        o_r, st = _retention(lg, r3(qr), r3(kr), r3(vr), r3(gr), gain, bias, l, states_in, l * bd,
                             SAMPLE_ROWS, sd, N_HEADS_R)
        xs = _outproj(oa.reshape(ms, W_A), o_r.reshape(ms, W_R), ga, gb, xs, l, wpa, wpr, wo, tm_s)
        xs = _ffn(xs, l, g2, w1b, w3b, w2b, tm_s)
        ss_l.append(st)

    y_prompt = xp.reshape(b, s, D_MODEL)
    y_sample = xs.reshape(bd, SAMPLE_ROWS, D_MODEL)[:, :sd]
    seq_minor = lambda a: a.reshape(depth, b, N_HEADS_A, HEAD_DIM_A, s).transpose(0, 1, 4, 2, 3)
    k_prompt, v_prompt = seq_minor(kv_p[0]), seq_minor(kv_p[1])
    heads5 = lambda a: a.reshape(depth, bd, SAMPLE_ROWS, N_HEADS_A, HEAD_DIM_A)[:, :, :sd]
    k_sample, v_sample = heads5(kv_s[0]), heads5(kv_s[1])
    return (y_prompt, y_sample, k_prompt, v_prompt, jnp.stack(sp_l),
            k_sample, v_sample, jnp.stack(ss_l))
```

```python
import functools

import jax
import jax.numpy as jnp
from jax import lax
from jax.experimental import pallas as pl
from jax.experimental.pallas import tpu as pltpu

F32 = jnp.float32
BF16 = jnp.bfloat16

D_MODEL = 1024
D_FF = 2816
N_HEADS_A = 8
HEAD_DIM_A = 64
MOBA_BLOCK = 256
MOBA_TOPK = 3
N_HEADS_R = 4
KEY_DIM_R = 128
VAL_DIM_R = 128
EPS = 1e-6
NEG_INF = -1e30
W_A = N_HEADS_A * HEAD_DIM_A
W_R = N_HEADS_R * KEY_DIM_R

LANES = 128
VMEM_LIMIT_BYTES = 56 * 1024 * 1024
SAMPLE_ROWS = 16
RET_CHUNK_PROMPT = 256
RET_HEADS_PROMPT = 2
PAGE_SLOTS = 32
PAGE_GROUP = 16


def _nt(a, b):
    return lax.dot_general(a, b, (((1,), (1,)), ((), ())), preferred_element_type=F32)


def _tn(a, b):
    return lax.dot_general(a, b, (((0,), (0,)), ((), ())), preferred_element_type=F32)


def _mm(a, b):
    return jnp.dot(a, b, preferred_element_type=F32)


def _rms(x, g):
    return x * lax.rsqrt(jnp.mean(x * x, axis=-1, keepdims=True) + EPS) * g


def _split_bf16(x):
    hi = x.astype(BF16)
    lo = (x - hi.astype(F32)).astype(BF16)
    return hi, lo


def _params(*sem):
    return pltpu.CompilerParams(dimension_semantics=sem, vmem_limit_bytes=VMEM_LIMIT_BYTES)


def _row_spec(tm, cols):
    return pl.BlockSpec((tm, cols), lambda i: (i, 0))


def _const_spec(shape):
    return pl.BlockSpec(shape, lambda i: (0,) * len(shape), pipeline_mode=pl.Buffered(1))


def _layer_spec(layer, shape):
    return pl.BlockSpec((None,) + shape, lambda i: (layer, 0, 0), pipeline_mode=pl.Buffered(1))


def _ffn_body(x_ref, g_ref, w1_ref, w3_ref, w2_ref, o_ref):
    x = x_ref[...]
    h = _rms(x, g_ref[...]).astype(BF16)
    a = _mm(h, w1_ref[...])
    b = _mm(h, w3_ref[...])
    act = (a * jax.nn.sigmoid(a) * b).astype(BF16)
    o_ref[...] = x + 0.5 * _mm(act, w2_ref[...])


def _ffn(x, layer, g, w1, w3, w2, tm):
    m = x.shape[0]
    return pl.pallas_call(
        _ffn_body,
        grid=(m // tm,),
        in_specs=[_row_spec(tm, D_MODEL), _layer_spec(layer, (1, D_MODEL)),
                  _layer_spec(layer, (D_MODEL, D_FF)), _layer_spec(layer, (D_MODEL, D_FF)),
                  _layer_spec(layer, (D_FF, D_MODEL))],
        out_specs=_row_spec(tm, D_MODEL),
        out_shape=jax.ShapeDtypeStruct((m, D_MODEL), F32),
        compiler_params=_params("parallel"),
        name="ffn",
    )(x, g, w1, w3, w2)


_IN_COLS = (W_A, W_A, W_A, W_R, W_R, W_R, W_R, D_MODEL, D_MODEL)
_IN_OFFS = tuple(sum(_IN_COLS[:i]) for i in range(len(_IN_COLS) + 1))
D_IN = _IN_OFFS[-1]


def _inproj_body(x_ref, g_ref, w_ref, gq_ref, gk_ref, e_ref, *rest, seq_major):
    n_out = 12 if seq_major else 11
    q_o, kf_o, kb_o, vf_o, vb_o, qr_o, kr_o, vr_o, gr_o, ga_o, gb_o = rest[-n_out:][:11]
    h = _rms(x_ref[...], g_ref[...]).astype(BF16)

    def proj(i):
        return _mm(h, w_ref[:, _IN_OFFS[i]:_IN_OFFS[i + 1]])

    def head_rms(z, gain):
        ss = _mm((z * z).astype(BF16), e_ref[...])
        return z * lax.rsqrt(ss * (1.0 / HEAD_DIM_A) + EPS) * gain

    q_o[...] = head_rms(proj(0), gq_ref[...]) * (HEAD_DIM_A ** -0.5)
    k = head_rms(proj(1), gk_ref[...])
    kb_o[...] = k.astype(BF16)
    v = proj(2)
    vb_o[...] = v.astype(BF16)
    if seq_major:
        kf_o[...] = k.T
        vf_o[...] = v.T
        km_o = rest[-1]
        for r in range(km_o.shape[0]):
            km_o[r] = jnp.mean(k[r * MOBA_BLOCK:(r + 1) * MOBA_BLOCK, :], axis=0, keepdims=True)
    else:
        kf_o[...] = k
        vf_o[...] = v
    qr_o[...] = proj(3).astype(BF16)
    kr_o[...] = (proj(4) * (KEY_DIM_R ** -0.5)).astype(BF16)
    vr_o[...] = proj(5).astype(BF16)
    gr_o[...] = proj(6)
    ga_o[...] = proj(7)
    gb_o[...] = proj(8)


def _inproj(x, layer, depth, g, w, gq, gk, e, tm, kv_stacks=None, seq_len=None):
    m = x.shape[0]
    outs = [(W_A, F32), (W_A, F32), (W_A, BF16), (W_A, F32), (W_A, BF16),
            (W_R, BF16), (W_R, BF16), (W_R, BF16), (W_R, F32), (D_MODEL, F32), (D_MODEL, F32)]
    stacked = (1, 3)
    out_specs = [_row_spec(tm, c) for c, _ in outs]
    out_shape = [jax.ShapeDtypeStruct((m, c), dt) for c, dt in outs]
    for n in stacked:
        if seq_len is None:
            out_specs[n] = pl.BlockSpec((None, tm, W_A), lambda i: (layer, i, 0))
            out_shape[n] = jax.ShapeDtypeStruct((depth, m, W_A), F32)
        else:
            tiles = seq_len // tm
            out_specs[n] = pl.BlockSpec((None, None, W_A, tm),
                                        lambda i: (layer, i // tiles, 0, i % tiles))
            out_shape[n] = jax.ShapeDtypeStruct((depth, m // seq_len, W_A, seq_len), F32)
    if seq_len is not None:
        out_specs.append(pl.BlockSpec((tm // MOBA_BLOCK, 1, W_A), lambda i: (i, 0, 0)))
        out_shape.append(jax.ShapeDtypeStruct((m // MOBA_BLOCK, 1, W_A), F32))
    in_specs = [_row_spec(tm, D_MODEL), _layer_spec(layer, (1, D_MODEL)),
                _layer_spec(layer, (D_MODEL, D_IN)), _layer_spec(layer, (1, W_A)),
                _layer_spec(layer, (1, W_A)), _const_spec((W_A, W_A))]
    args = [x, g, w, gq, gk, e]
    aliases = {}
    if kv_stacks is not None:
        for n, stack in zip(stacked, kv_stacks):
            aliases[len(args)] = n
            in_specs.append(pl.BlockSpec(memory_space=pl.ANY))
            args.append(stack)
    return pl.pallas_call(
        functools.partial(_inproj_body, seq_major=seq_len is not None),
        grid=(m // tm,),
        in_specs=in_specs,
        out_specs=out_specs,
        out_shape=out_shape,
        input_output_aliases=aliases,
        compiler_params=_params("parallel"),
        name="inproj",
    )(*args)


def _outproj_body(oa_ref, or_ref, ga_ref, gb_ref, x_ref, wpa_ref, wpr_ref, wo_ref, o_ref):
    a = _mm(oa_ref[...], wpa_ref[...])
    r = _mm(or_ref[...], wpr_ref[...])
    merged = jax.nn.sigmoid(ga_ref[...]) * a + jax.nn.sigmoid(gb_ref[...]) * r
    o_ref[...] = x_ref[...] + _mm(merged.astype(BF16), wo_ref[...])


def _outproj(oa, o_r, ga, gb, x, layer, wpa, wpr, wo, tm):
    m = x.shape[0]
    return pl.pallas_call(
        _outproj_body,
        grid=(m // tm,),
        in_specs=[_row_spec(tm, W_A), _row_spec(tm, W_R), _row_spec(tm, D_MODEL),
                  _row_spec(tm, D_MODEL), _row_spec(tm, D_MODEL),
                  _layer_spec(layer, (W_A, D_MODEL)), _layer_spec(layer, (W_R, D_MODEL)),
                  _layer_spec(layer, (D_MODEL, D_MODEL))],
        out_specs=_row_spec(tm, D_MODEL),
        out_shape=jax.ShapeDtypeStruct((m, D_MODEL), F32),
        compiler_params=_params("parallel"),
        name="outproj",
    )(oa, o_r, ga, gb, x, wpa, wpr, wo)


def _rank_select(gate, lane, n_valid_mask, candidates):
    gate = jnp.where(n_valid_mask, gate, NEG_INF)
    cnt = jnp.zeros(gate.shape, F32)
    for c in candidates:
        gc = gate[:, c:c + 1]
        beats = (gc > gate) | ((gc == gate) & (lane > c))
        cnt = cnt + jnp.where(beats, 1.0, 0.0)
    return jnp.where((cnt < MOBA_TOPK) & n_valid_mask, 1.0, 0.0)


def _moba_tables(s):
    nb = s // MOBA_BLOCK
    pos = jnp.arange(s)
    blk = (pos // MOBA_BLOCK).astype(F32)
    within = (pos % MOBA_BLOCK).astype(F32)
    ind = (pos[:, None] // MOBA_BLOCK == jnp.arange(nb)[None, :]).astype(F32)
    slopes = 2.0 ** (-8.0 * (jnp.arange(N_HEADS_A, dtype=F32) + 1.0) / N_HEADS_A)

    def half(slope):
        cols = [ind, (slope * MOBA_BLOCK * blk)[:, None], (slope * within)[:, None],
                jnp.zeros((s, HEAD_DIM_A - nb - 2), F32)]
        return jnp.concatenate(cols, axis=1)

    pairs = [jnp.concatenate([half(slopes[2 * p + 1]), half(slopes[2 * p])], axis=1)
             for p in range(N_HEADS_A // 2)]
    return jnp.stack(pairs).astype(BF16)


def _moba_prompt_body(q_ref, k_ref, v_ref, km_ref, tab_ref, o_ref, qa_ref, qb_ref, ka_ref, kb_ref,
                      *, nb):
    blk = MOBA_BLOCK
    s_len = nb * blk
    q2 = q_ref[0]
    lane = lax.broadcasted_iota(jnp.int32, (1, LANES), 1)
    lo_half = lane < HEAD_DIM_A

    km = km_ref[0]
    km_parts = []
    for part in (jnp.where(lo_half, km, 0.0), jnp.where(lo_half, 0.0, km)):
        km_parts.extend(_split_bf16(part))
    kms = jnp.concatenate(km_parts, axis=0)
    q_hi, q_lo = _split_bf16(q2)
    r_hi = _nt(kms, q_hi)
    r_lo = _nt(kms, q_lo)
    gates = [r_hi[0:nb] + r_hi[nb:2 * nb] + r_lo[0:nb],
             r_hi[2 * nb:3 * nb] + r_hi[3 * nb:4 * nb] + r_lo[2 * nb:3 * nb]]

    brow = lax.broadcasted_iota(jnp.int32, (nb, s_len), 0)
    own = lax.broadcasted_iota(jnp.int32, (nb, s_len), 1) // blk
    past = brow < own

    def block_bias(gate):
        g = jnp.where(past, gate, NEG_INF)
        cnt = jnp.zeros((nb, s_len), F32)
        for c in range(nb):
            gc = g[c:c + 1, :]
            beats = (gc > g) | ((gc == g) & (brow > c))
            cnt = cnt + jnp.where(beats, 1.0, 0.0)
        allowed = ((cnt < MOBA_TOPK) & past) | (brow == own)
        return jnp.where(allowed, 0.0, NEG_INF)

    ones2 = jnp.where(lax.broadcasted_iota(jnp.int32, (8, s_len), 0) < 2, 1.0, 0.0)
    pad = jnp.zeros((HEAD_DIM_A - nb - 8, s_len), F32)
    ext_t = jnp.concatenate([block_bias(gates[1]), ones2, pad, block_bias(gates[0]), ones2, pad], axis=0)
    ext = ext_t.T
    qa_ref[...] = jnp.where(lo_half, q2, ext).astype(BF16)
    qb_ref[...] = jnp.where(lo_half, ext, q2).astype(BF16)
    k2 = k_ref[0]
    tab = tab_ref[0]
    ka_ref[...] = jnp.where(lo_half, k2, tab)
    kb_ref[...] = jnp.where(lo_half, tab, k2)

    row = lax.broadcasted_iota(jnp.int32, (blk, blk), 0)
    col = lax.broadcasted_iota(jnp.int32, (blk, blk), 1)
    causal = col <= row

    for j in range(nb):
        lo, hi = j * blk, (j + 1) * blk
        outs = []
        for qx_ref, kx_ref in ((qa_ref, ka_ref), (qb_ref, kb_ref)):
            qx = qx_ref[lo:hi, :]
            s_own = jnp.where(causal, _nt(qx, kx_ref[lo:hi, :]), NEG_INF)
            m = jnp.max(s_own, axis=-1, keepdims=True)
            if j > 0:
                s_past = _nt(qx, kx_ref[0:lo, :])
                m = jnp.maximum(m, jnp.max(s_past, axis=-1, keepdims=True))
            p_own = jnp.exp(s_own - m)
            l = jnp.sum(p_own, axis=-1, keepdims=True)
            acc = _mm(p_own.astype(BF16), v_ref[0, lo:hi, :])
            if j > 0:
                p_past = jnp.exp(s_past - m)
                l = l + jnp.sum(p_past, axis=-1, keepdims=True)
                acc = acc + _mm(p_past.astype(BF16), v_ref[0, 0:lo, :])
            outs.append(acc / l)
        o_ref[0, lo:hi, :] = jnp.where(lo_half, outs[0], outs[1]).astype(o_ref.dtype)


def _moba_prompt(q, kb, vb, km, tables):
    b, s, _ = q.shape
    nb = s // MOBA_BLOCK
    assert nb == 8, "block-indicator columns are laid out for 8 MoBA blocks"
    npair = N_HEADS_A // 2
    seq_spec = pl.BlockSpec((1, s, LANES), lambda bi, p: (bi, 0, p))
    return pl.pallas_call(
        functools.partial(_moba_prompt_body, nb=nb),
        grid=(b, npair),
        in_specs=[seq_spec, seq_spec, seq_spec,
                  pl.BlockSpec((1, nb, LANES), lambda bi, p: (bi, 0, p)),
                  pl.BlockSpec((1, s, LANES), lambda bi, p: (p, 0, 0))],
        out_specs=seq_spec,
        out_shape=jax.ShapeDtypeStruct((b, s, W_A), BF16),
        scratch_shapes=[pltpu.VMEM((s, LANES), BF16)] * 4,
        compiler_params=_params("parallel", "parallel"),
        name="moba_prompt",
    )(q, kb, vb, km, tables)


def _moba_sample_body(pt_ref, q_ref, kn_ref, vn_ref, kt_hbm, vt_hbm, o_ref, buf, sem, s_ref, p_ref,
                      *, layer, n_real, past_len):
    b = pl.program_id(0)
    n_seq = pl.num_programs(0)
    n_pages = s_ref.shape[0]
    ring = PAGE_SLOTS
    rows = N_HEADS_A * n_real

    def page_copy(hbm, seq, pg):
        slot = pg % ring
        return pltpu.make_async_copy(hbm.at[layer, pt_ref[seq, pg]], buf.at[slot], sem.at[slot])

    @pl.when(b == 0)
    def _():
        for pg in range(ring):
            page_copy(kt_hbm, b, pg).start()

    rowi = lax.broadcasted_iota(jnp.int32, (rows, W_A), 0)
    lanei = lax.broadcasted_iota(jnp.int32, (rows, W_A), 1)
    head_mask = (lanei >> 6) == (rowi & (N_HEADS_A - 1))
    q = q_ref[0]
    q_rep = jnp.concatenate(
        [jnp.broadcast_to(q[t:t + 1, :], (N_HEADS_A, W_A)) for t in range(n_real)], axis=0)
    q_hi, q_lo = _split_bf16(jnp.where(head_mask, q_rep, 0.0))
    q2 = jnp.concatenate([q_hi, q_lo], axis=0)

    def raw_scores(keys_t=None, keys=None):
        s2 = _mm(q2, keys_t) if keys_t is not None else _nt(q2, keys)
        return s2[:rows] + s2[rows:]

    lane = lax.broadcasted_iota(jnp.int32, (1, LANES), 1)

    def start_group(hbm, seq, first):
        for pg in range(first, first + PAGE_GROUP):
            page_copy(hbm, seq, pg).start()

    def wait_group(hbm, first):
        for pg in range(first, first + PAGE_GROUP):
            page_copy(hbm, b, pg).wait()

    gsum = jnp.zeros((rows, LANES), F32)
    for first in range(0, n_pages, PAGE_GROUP):
        wait_group(kt_hbm, first)
        for pg in range(first, first + PAGE_GROUP):
            s = raw_scores(keys_t=buf[pg % ring].astype(BF16))
            s_ref[pg] = s
            gsum = jnp.where(lane == pg, jnp.sum(s, axis=-1, keepdims=True), gsum)
        if first + ring < n_pages:
            start_group(kt_hbm, b, first + ring)
        else:
            start_group(vt_hbm, b, first + ring - n_pages)

    gate = (gsum + pltpu.roll(gsum, LANES - 1, 1)) * (1.0 / MOBA_BLOCK)
    valid = ((lane & 1) == 0) & (lane < n_pages)
    sel = _rank_select(gate, lane, valid, range(0, n_pages, 2))

    rcol = lax.broadcasted_iota(jnp.int32, (rows, 1), 0)
    slope = jnp.exp2(-((rcol & (N_HEADS_A - 1)) + 1).astype(F32))
    qpos = past_len + (rcol >> 3)

    m_lanes = jnp.full((rows, LANES), NEG_INF, F32)
    for blk in range(n_pages // 2):
        chosen = sel[:, 2 * blk:2 * blk + 1] > 0.5
        for pg in (2 * blk, 2 * blk + 1):
            kpos = pg * LANES + lane
            s = s_ref[pg] - slope * (qpos - kpos).astype(F32)
            s = jnp.where(chosen, s, NEG_INF)
            s_ref[pg] = s
            m_lanes = jnp.maximum(m_lanes, s)
    tnew = lax.broadcasted_iota(jnp.int32, (1, SAMPLE_ROWS), 1)
    dist = (rcol >> 3) - tnew
    s_own = raw_scores(keys=kn_ref[0]) - slope * dist.astype(F32)
    s_own = jnp.where(dist >= 0, s_own, NEG_INF)
    m = jnp.maximum(jnp.max(m_lanes, axis=-1, keepdims=True), jnp.max(s_own, axis=-1, keepdims=True))

    l_lanes = jnp.zeros((rows, LANES), F32)
    for pg in range(n_pages):
        p = jnp.exp(s_ref[pg] - m)
        l_lanes = l_lanes + p
        p_ref[pg] = p.astype(BF16)
    p_own = jnp.exp(s_own - m)
    l = jnp.sum(l_lanes, axis=-1, keepdims=True) + jnp.sum(p_own, axis=-1, keepdims=True)
    acc = _mm(p_own.astype(BF16), vn_ref[0])

    for first in range(0, n_pages, PAGE_GROUP):
        wait_group(vt_hbm, first)
        for pg in range(first, first + PAGE_GROUP):
            acc = acc + _nt(p_ref[pg], buf[pg % ring].astype(BF16))
        if first + ring < n_pages:
            start_group(vt_hbm, b, first + ring)
        else:
            @pl.when(b + 1 < n_seq)
            def _():
                start_group(kt_hbm, b + 1, first + ring - n_pages)

    o = jnp.where(head_mask, acc / l, 0.0)
    out_rows = [jnp.sum(o[N_HEADS_A * t:N_HEADS_A * (t + 1), :], axis=0, keepdims=True)
                for t in range(n_real)]
    out_rows.append(jnp.zeros((SAMPLE_ROWS - n_real, W_A), F32))
    o_ref[0] = jnp.concatenate(out_rows, axis=0).astype(o_ref.dtype)


def _moba_sample(page_table, q, kn, vn, cache_kt, cache_vt, layer, n_real):
    bd, n_pages = page_table.shape
    page = cache_kt.shape[-1]
    assert page == LANES and MOBA_BLOCK == 2 * page
    assert PAGE_SLOTS <= n_pages <= LANES and n_pages % PAGE_GROUP == 0 == PAGE_SLOTS % PAGE_GROUP
    rows = N_HEADS_A * n_real
    tok_spec = pl.BlockSpec((1, SAMPLE_ROWS, W_A), lambda b, pt: (b, 0, 0))
    hbm_spec = pl.BlockSpec(memory_space=pl.ANY)
    grid_spec = pltpu.PrefetchScalarGridSpec(
        num_scalar_prefetch=1,
        grid=(bd,),
        in_specs=[tok_spec, tok_spec, tok_spec, hbm_spec, hbm_spec],
        out_specs=tok_spec,
        scratch_shapes=[pltpu.VMEM((PAGE_SLOTS, W_A, page), F32),
                        pltpu.SemaphoreType.DMA((PAGE_SLOTS,)),
                        pltpu.VMEM((n_pages, rows, LANES), F32),
                        pltpu.VMEM((n_pages, rows, LANES), BF16)],
    )
    return pl.pallas_call(
        functools.partial(_moba_sample_body, layer=layer, n_real=n_real, past_len=n_pages * page),
        grid_spec=grid_spec,
        out_shape=jax.ShapeDtypeStruct((bd, SAMPLE_ROWS, W_A), BF16),
        compiler_params=_params("arbitrary"),
        name="moba_sample",
    )(page_table, q, kn, vn, cache_kt, cache_vt)


def _log_decay_rows():
    log_decay = jnp.log(1.0 - 2.0 ** (-5.0 - jnp.arange(N_HEADS_R, dtype=F32)))
    return jnp.broadcast_to(log_decay.reshape(N_HEADS_R, 1, 1), (N_HEADS_R, 1, LANES))


def _retention_body(lg_ref, q_ref, k_ref, v_ref, gr_ref, gain_ref, bias_ref, s0_ref,
                    o_ref, s_out_ref, *, chunk, n_real, n_chunks, heads):
    hd = KEY_DIM_R
    ii = lax.broadcasted_iota(jnp.int32, (chunk, chunk), 0)
    jj = lax.broadcasted_iota(jnp.int32, (chunk, chunk), 1)
    diff = (ii - jj).astype(F32)
    icol = lax.broadcasted_iota(jnp.int32, (chunk, 1), 0).astype(F32)
    for h in range(heads):
        cols = slice(h * hd, (h + 1) * hd)
        lg = lg_ref[h][:, 0:1]
        dmat = jnp.where(diff >= 0, jnp.exp(lg * jnp.maximum(diff, 0.0)), 0.0)
        qdec = jnp.exp(lg * (icol + 1.0))
        kdec = jnp.where(icol < n_real, jnp.exp(lg * (n_real - 1.0 - icol)), 0.0)
        chunk_decay = jnp.exp(lg * float(n_real))
        gain = gain_ref[:, cols]
        bias = bias_ref[:, cols]
        state = s0_ref[0, h]
        for c in range(n_chunks):
            rows = slice(c * chunk, (c + 1) * chunk)
            qc = q_ref[0, rows, cols]
            kc = k_ref[0, rows, cols]
            vc = v_ref[0, rows, cols]
            inner = _nt(qc, kc) * dmat
            o = _mm(inner.astype(BF16), vc) + _mm(qc, state.astype(BF16)) * qdec
            kd = (kc.astype(F32) * kdec).astype(BF16)
            state = chunk_decay * state + _tn(kd, vc)
            mu = jnp.mean(o, axis=-1, keepdims=True)
            d = o - mu
            var = jnp.mean(d * d, axis=-1, keepdims=True)
            y = d * lax.rsqrt(var + EPS) * gain + bias
            gr = gr_ref[0, rows, cols]
            o_ref[0, rows, cols] = (y * (gr * jax.nn.sigmoid(gr))).astype(o_ref.dtype)
        s_out_ref[0, h] = state


def _retention(lg, q, k, v, gr, gain, bias, layer, state0, state_offset, chunk, n_real, heads):
    b, s, _ = q.shape
    hd = KEY_DIM_R
    width = heads * hd
    seq_spec = pl.BlockSpec((1, s, width), lambda bi, g: (bi, 0, g))
    vec_spec = pl.BlockSpec((None, 1, width), lambda bi, g: (layer, 0, g))
    st_spec = pl.BlockSpec((1, heads, hd, hd), lambda bi, g: (bi, g, 0, 0))
    st_in_spec = pl.BlockSpec((1, heads, hd, hd), lambda bi, g: (state_offset + bi, g, 0, 0))
    return pl.pallas_call(
        functools.partial(_retention_body, chunk=chunk, n_real=n_real, n_chunks=s // chunk,
                          heads=heads),
        grid=(b, N_HEADS_R // heads),
        in_specs=[pl.BlockSpec((heads, 1, LANES), lambda bi, g: (g, 0, 0)),
                  seq_spec, seq_spec, seq_spec, seq_spec, vec_spec, vec_spec, st_in_spec],
        out_specs=[seq_spec, st_spec],
        out_shape=[jax.ShapeDtypeStruct((b, s, W_R), BF16),
                   jax.ShapeDtypeStruct((b, N_HEADS_R, hd, hd), F32)],
        compiler_params=_params("parallel", "parallel"),
        name="retention",
    )(lg, q, k, v, gr, gain, bias, state0)


def kernel(x_prompt, x_sample, cache_k, cache_v, state_ret, page_table, g_ffn1, w1_ffn1, w3_ffn1,
           w2_ffn1, g_mix, w_in, g_q, g_k, gn_gain, gn_bias, w_pa, w_pr, w_o, g_ffn2, w1_ffn2,
           w3_ffn2, w2_ffn2):
    b, s, _ = x_prompt.shape
    bd, sd, _ = x_sample.shape
    depth = w_in.shape[0]
    n_pool, page = cache_k.shape[1], cache_k.shape[2]
    mp, ms = b * s, bd * SAMPLE_ROWS
    tm_p, tm_s = 512, ms

    xp = x_prompt.reshape(mp, D_MODEL)
    xs = jnp.pad(x_sample, ((0, 0), (0, SAMPLE_ROWS - sd), (0, 0))).reshape(ms, D_MODEL)

    cache_kt = jnp.transpose(cache_k, (0, 1, 3, 4, 2)).reshape(depth, n_pool, W_A, page)
    cache_vt = jnp.transpose(cache_v, (0, 1, 3, 4, 2)).reshape(depth, n_pool, W_A, page)

    tables = _moba_tables(s)
    lg = _log_decay_rows()
    head_ones = jnp.kron(jnp.eye(N_HEADS_A, dtype=F32),
                         jnp.ones((HEAD_DIM_A, HEAD_DIM_A), F32)).astype(BF16)
    zero_state = jnp.zeros((b, N_HEADS_R, KEY_DIM_R, VAL_DIM_R), F32)
    states_in = state_ret.reshape(depth * bd, N_HEADS_R, KEY_DIM_R, VAL_DIM_R)

    vec = lambda a: a.reshape(depth, 1, -1)
    g1, gm, g2, gain, bias = vec(g_ffn1), vec(g_mix), vec(g_ffn2), vec(gn_gain), vec(gn_bias)
    gq, gk = vec(jnp.tile(g_q, (1, N_HEADS_A))), vec(jnp.tile(g_k, (1, N_HEADS_A)))
    w1a, w3a, w2a = w1_ffn1.astype(BF16), w3_ffn1.astype(BF16), w2_ffn1.astype(BF16)
    w1b, w3b, w2b = w1_ffn2.astype(BF16), w3_ffn2.astype(BF16), w2_ffn2.astype(BF16)
    win, wpa, wpr, wo = w_in.astype(BF16), w_pa.astype(BF16), w_pr.astype(BF16), w_o.astype(BF16)

    kv_p = kv_s = None
    sp_l, ss_l = [], []
    for l in range(depth):
        xp = _ffn(xp, l, g1, w1a, w3a, w2a, tm_p)
        q, kf, kb, vf, vb, qr, kr, vr, gr, ga, gb, km = _inproj(
            xp, l, depth, gm, win, gq, gk, head_ones, tm_p, kv_p, seq_len=s)
        kv_p = (kf, vf)
        r3 = lambda a: a.reshape(b, s, a.shape[-1])
        oa = _moba_prompt(r3(q), r3(kb), r3(vb), km.reshape(b, s // MOBA_BLOCK, W_A), tables)
        o_r, st = _retention(lg, r3(qr), r3(kr), r3(vr), r3(gr), gain, bias, l, zero_state, 0,
                             RET_CHUNK_PROMPT, RET_CHUNK_PROMPT, RET_HEADS_PROMPT)
        xp = _outproj(oa.reshape(mp, W_A), o_r.reshape(mp, W_R), ga, gb, xp, l, wpa, wpr, wo, tm_p)
        xp = _ffn(xp, l, g2, w1b, w3b, w2b, tm_p)
        sp_l.append(st)

        xs = _ffn(xs, l, g1, w1a, w3a, w2a, tm_s)
        q, kf, kb, vf, vb, qr, kr, vr, gr, ga, gb = _inproj(
            xs, l, depth, gm, win, gq, gk, head_ones, tm_s, kv_s)
        kv_s = (kf, vf)
        r3 = lambda a: a.reshape(bd, SAMPLE_ROWS, a.shape[-1])
        oa = _moba_sample(page_table, r3(q), r3(kb), r3(vb), cache_kt, cache_vt, l, sd)
        o_r, st = _retention(lg, r3(qr), r3(kr), r3(vr), r3(gr), gain, bias, l, states_in, l * bd,
                             SAMPLE_ROWS, sd, N_HEADS_R)
        xs = _outproj(oa.reshape(ms, W_A), o_r.reshape(ms, W_R), ga, gb, xs, l, wpa, wpr, wo, tm_s)
        xs = _ffn(xs, l, g2, w1b, w3b, w2b, tm_s)
        ss_l.append(st)

    y_prompt = xp.reshape(b, s, D_MODEL)
    y_sample = xs.reshape(bd, SAMPLE_ROWS, D_MODEL)[:, :sd]
    seq_minor = lambda a: a.reshape(depth, b, N_HEADS_A, HEAD_DIM_A, s).transpose(0, 1, 4, 2, 3)
    k_prompt, v_prompt = seq_minor(kv_p[0]), seq_minor(kv_p[1])
    heads5 = lambda a: a.reshape(depth, bd, SAMPLE_ROWS, N_HEADS_A, HEAD_DIM_A)[:, :, :sd]
    k_sample, v_sample = heads5(kv_s[0]), heads5(kv_s[1])
    return (y_prompt, y_sample, k_prompt, v_prompt, jnp.stack(sp_l),
            k_sample, v_sample, jnp.stack(ss_l))
```

```python
import functools

import jax
import jax.numpy as jnp
from jax import lax
from jax.experimental import pallas as pl
from jax.experimental.pallas import tpu as pltpu

F32 = jnp.float32
BF16 = jnp.bfloat16

D_MODEL = 1024
D_FF = 2816
N_HEADS_A = 8
HEAD_DIM_A = 64
MOBA_BLOCK = 256
MOBA_TOPK = 3
N_HEADS_R = 4
KEY_DIM_R = 128
VAL_DIM_R = 128
EPS = 1e-6
NEG_INF = -1e30
W_A = N_HEADS_A * HEAD_DIM_A
W_R = N_HEADS_R * KEY_DIM_R

LANES = 128
VMEM_LIMIT_BYTES = 56 * 1024 * 1024
SAMPLE_ROWS = 16
RET_CHUNK_PROMPT = 256
RET_HEADS_PROMPT = 2
PAGE_SLOTS = 32
PAGE_GROUP = 16
FFN_CHUNK = 256


def _nt(a, b):
    return lax.dot_general(a, b, (((1,), (1,)), ((), ())), preferred_element_type=F32)


def _tn(a, b):
    return lax.dot_general(a, b, (((0,), (0,)), ((), ())), preferred_element_type=F32)


def _mm(a, b):
    return jnp.dot(a, b, preferred_element_type=F32)


def _rms(x, g):
    return x * lax.rsqrt(jnp.mean(x * x, axis=-1, keepdims=True) + EPS) * g


def _split_bf16(x):
    hi = x.astype(BF16)
    lo = (x - hi.astype(F32)).astype(BF16)
    return hi, lo


def _params(*sem):
    return pltpu.CompilerParams(dimension_semantics=sem, vmem_limit_bytes=VMEM_LIMIT_BYTES)


def _row_spec(tm, cols):
    return pl.BlockSpec((tm, cols), lambda i: (i, 0))


def _const_spec(shape):
    return pl.BlockSpec(shape, lambda i: (0,) * len(shape), pipeline_mode=pl.Buffered(1))


def _layer_spec(layer, shape):
    return pl.BlockSpec((None,) + shape, lambda i: (layer, 0, 0), pipeline_mode=pl.Buffered(1))


def _ffn_body(x_ref, g_ref, w1_ref, w3_ref, w2_ref, o_ref):
    x = x_ref[...]
    h = _rms(x, g_ref[...]).astype(BF16)
    a = _mm(h, w1_ref[...])
    b = _mm(h, w3_ref[...])
    act = (a * jax.nn.sigmoid(a) * b).astype(BF16)
    o_ref[...] = x + 0.5 * _mm(act, w2_ref[...])


def _ffn(x, layer, g, w1, w3, w2, tm):
    m = x.shape[0]
    return pl.pallas_call(
        _ffn_body,
        grid=(m // tm,),
        in_specs=[_row_spec(tm, D_MODEL), _layer_spec(layer, (1, D_MODEL)),
                  _layer_spec(layer, (D_MODEL, D_FF)), _layer_spec(layer, (D_MODEL, D_FF)),
                  _layer_spec(layer, (D_FF, D_MODEL))],
        out_specs=_row_spec(tm, D_MODEL),
        out_shape=jax.ShapeDtypeStruct((m, D_MODEL), F32),
        compiler_params=_params("parallel"),
        name="ffn",
    )(x, g, w1, w3, w2)


_IN_COLS = (W_A, W_A, W_A, W_R, W_R, W_R, W_R, D_MODEL, D_MODEL)
_IN_OFFS = tuple(sum(_IN_COLS[:i]) for i in range(len(_IN_COLS) + 1))
D_IN = _IN_OFFS[-1]


def _inproj_body(x_ref, g_ref, w_ref, gq_ref, gk_ref, e_ref, *rest, seq_major):
    n_out = 12 if seq_major else 11
    q_o, kf_o, kb_o, vf_o, vb_o, qr_o, kr_o, vr_o, gr_o, ga_o, gb_o = rest[-n_out:][:11]
    h = _rms(x_ref[...], g_ref[...]).astype(BF16)

    def proj(i):
        return _mm(h, w_ref[:, _IN_OFFS[i]:_IN_OFFS[i + 1]])

    def head_rms(z, gain):
        ss = _mm((z * z).astype(BF16), e_ref[...])
        return z * lax.rsqrt(ss * (1.0 / HEAD_DIM_A) + EPS) * gain

    q_o[...] = head_rms(proj(0), gq_ref[...]) * (HEAD_DIM_A ** -0.5)
    k = head_rms(proj(1), gk_ref[...])
    kb_o[...] = k.astype(BF16)
    v = proj(2)
    vb_o[...] = v.astype(BF16)
    if seq_major:
        kf_o[...] = k.T
        vf_o[...] = v.T
        km_o = rest[-1]
        for r in range(km_o.shape[0]):
            km_o[r] = jnp.mean(k[r * MOBA_BLOCK:(r + 1) * MOBA_BLOCK, :], axis=0, keepdims=True)
    else:
        kf_o[...] = k
        vf_o[...] = v
    qr_o[...] = proj(3).astype(BF16)
    kr_o[...] = (proj(4) * (KEY_DIM_R ** -0.5)).astype(BF16)
    vr_o[...] = proj(5).astype(BF16)
    gr_o[...] = proj(6)
    ga_o[...] = proj(7)
    gb_o[...] = proj(8)


def _inproj(x, layer, depth, g, w, gq, gk, e, tm, kv_stacks=None, seq_len=None):
    m = x.shape[0]
    outs = [(W_A, F32), (W_A, F32), (W_A, BF16), (W_A, F32), (W_A, BF16),
            (W_R, BF16), (W_R, BF16), (W_R, BF16), (W_R, F32), (D_MODEL, F32), (D_MODEL, F32)]
    stacked = (1, 3)
    out_specs = [_row_spec(tm, c) for c, _ in outs]
    out_shape = [jax.ShapeDtypeStruct((m, c), dt) for c, dt in outs]
    for n in stacked:
        if seq_len is None:
            out_specs[n] = pl.BlockSpec((None, tm, W_A), lambda i: (layer, i, 0))
            out_shape[n] = jax.ShapeDtypeStruct((depth, m, W_A), F32)
        else:
            tiles = seq_len // tm
            out_specs[n] = pl.BlockSpec((None, None, W_A, tm),
                                        lambda i: (layer, i // tiles, 0, i % tiles))
            out_shape[n] = jax.ShapeDtypeStruct((depth, m // seq_len, W_A, seq_len), F32)
    if seq_len is not None:
        out_specs.append(pl.BlockSpec((tm // MOBA_BLOCK, 1, W_A), lambda i: (i, 0, 0)))
        out_shape.append(jax.ShapeDtypeStruct((m // MOBA_BLOCK, 1, W_A), F32))
    in_specs = [_row_spec(tm, D_MODEL), _layer_spec(layer, (1, D_MODEL)),
                _layer_spec(layer, (D_MODEL, D_IN)), _layer_spec(layer, (1, W_A)),
                _layer_spec(layer, (1, W_A)), _const_spec((W_A, W_A))]
    args = [x, g, w, gq, gk, e]
    aliases = {}
    if kv_stacks is not None:
        for n, stack in zip(stacked, kv_stacks):
            aliases[len(args)] = n
            in_specs.append(pl.BlockSpec(memory_space=pl.ANY))
            args.append(stack)
    return pl.pallas_call(
        functools.partial(_inproj_body, seq_major=seq_len is not None),
        grid=(m // tm,),
        in_specs=in_specs,
        out_specs=out_specs,
        out_shape=out_shape,
        input_output_aliases=aliases,
        compiler_params=_params("parallel"),
        name="inproj",
    )(*args)


def _outproj_body(oa_ref, or_ref, ga_ref, gb_ref, x_ref, wpa_ref, wpr_ref, wo_ref, o_ref):
    a = _mm(oa_ref[...], wpa_ref[...])
    r = _mm(or_ref[...], wpr_ref[...])
    merged = jax.nn.sigmoid(ga_ref[...]) * a + jax.nn.sigmoid(gb_ref[...]) * r
    o_ref[...] = x_ref[...] + _mm(merged.astype(BF16), wo_ref[...])


def _outproj(oa, o_r, ga, gb, x, layer, wpa, wpr, wo, tm):
    m = x.shape[0]
    return pl.pallas_call(
        _outproj_body,
        grid=(m // tm,),
        in_specs=[_row_spec(tm, W_A), _row_spec(tm, W_R), _row_spec(tm, D_MODEL),
                  _row_spec(tm, D_MODEL), _row_spec(tm, D_MODEL),
                  _layer_spec(layer, (W_A, D_MODEL)), _layer_spec(layer, (W_R, D_MODEL)),
                  _layer_spec(layer, (D_MODEL, D_MODEL))],
        out_specs=_row_spec(tm, D_MODEL),
        out_shape=jax.ShapeDtypeStruct((m, D_MODEL), F32),
        compiler_params=_params("parallel"),
        name="outproj",
    )(oa, o_r, ga, gb, x, wpa, wpr, wo)


def _rank_select(gate, lane, n_valid_mask, candidates):
    gate = jnp.where(n_valid_mask, gate, NEG_INF)
    cnt = jnp.zeros(gate.shape, F32)
    for c in candidates:
        gc = gate[:, c:c + 1]
        beats = (gc > gate) | ((gc == gate) & (lane > c))
        cnt = cnt + jnp.where(beats, 1.0, 0.0)
    return jnp.where((cnt < MOBA_TOPK) & n_valid_mask, 1.0, 0.0)


def _moba_tables(s):
    nb = s // MOBA_BLOCK
    pos = jnp.arange(s)
    blk = (pos // MOBA_BLOCK).astype(F32)
    within = (pos % MOBA_BLOCK).astype(F32)
    ind = (pos[:, None] // MOBA_BLOCK == jnp.arange(nb)[None, :]).astype(F32)
    slopes = 2.0 ** (-8.0 * (jnp.arange(N_HEADS_A, dtype=F32) + 1.0) / N_HEADS_A)

    def half(slope):
        cols = [ind, (slope * MOBA_BLOCK * blk)[:, None], (slope * within)[:, None],
                jnp.zeros((s, HEAD_DIM_A - nb - 2), F32)]
        return jnp.concatenate(cols, axis=1)

    pairs = [jnp.concatenate([half(slopes[2 * p + 1]), half(slopes[2 * p])], axis=1)
             for p in range(N_HEADS_A // 2)]
    return jnp.stack(pairs).astype(BF16)


def _moba_prompt_body(q_ref, k_ref, v_ref, km_ref, tab_ref, o_ref, qa_ref, qb_ref, ka_ref, kb_ref,
                      *, nb):
    blk = MOBA_BLOCK
    s_len = nb * blk
    q2 = q_ref[0]
    lane = lax.broadcasted_iota(jnp.int32, (1, LANES), 1)
    lo_half = lane < HEAD_DIM_A

    km = km_ref[0]
    km_parts = []
    for part in (jnp.where(lo_half, km, 0.0), jnp.where(lo_half, 0.0, km)):
        km_parts.extend(_split_bf16(part))
    kms = jnp.concatenate(km_parts, axis=0)
    q_hi, q_lo = _split_bf16(q2)
    r_hi = _nt(kms, q_hi)
    r_lo = _nt(kms, q_lo)
    gates = [r_hi[0:nb] + r_hi[nb:2 * nb] + r_lo[0:nb],
             r_hi[2 * nb:3 * nb] + r_hi[3 * nb:4 * nb] + r_lo[2 * nb:3 * nb]]

    brow = lax.broadcasted_iota(jnp.int32, (nb, s_len), 0)
    own = lax.broadcasted_iota(jnp.int32, (nb, s_len), 1) // blk
    past = brow < own

    def block_bias(gate):
        g = jnp.where(past, gate, NEG_INF)
        cnt = jnp.zeros((nb, s_len), F32)
        for c in range(nb):
            gc = g[c:c + 1, :]
            beats = (gc > g) | ((gc == g) & (brow > c))
            cnt = cnt + jnp.where(beats, 1.0, 0.0)
        allowed = ((cnt < MOBA_TOPK) & past) | (brow == own)
        return jnp.where(allowed, 0.0, NEG_INF)

    ones2 = jnp.where(lax.broadcasted_iota(jnp.int32, (8, s_len), 0) < 2, 1.0, 0.0)
    pad = jnp.zeros((HEAD_DIM_A - nb - 8, s_len), F32)
    ext_t = jnp.concatenate([block_bias(gates[1]), ones2, pad, block_bias(gates[0]), ones2, pad], axis=0)
    ext = ext_t.T
    qa_ref[...] = jnp.where(lo_half, q2, ext).astype(BF16)
    qb_ref[...] = jnp.where(lo_half, ext, q2).astype(BF16)
    k2 = k_ref[0]
    tab = tab_ref[0]
    ka_ref[...] = jnp.where(lo_half, k2, tab)
    kb_ref[...] = jnp.where(lo_half, tab, k2)

    row = lax.broadcasted_iota(jnp.int32, (blk, blk), 0)
    col = lax.broadcasted_iota(jnp.int32, (blk, blk), 1)
    causal = col <= row

    for j in range(nb):
        lo, hi = j * blk, (j + 1) * blk
        outs = []
        for qx_ref, kx_ref in ((qa_ref, ka_ref), (qb_ref, kb_ref)):
            qx = qx_ref[lo:hi, :]
            s_own = jnp.where(causal, _nt(qx, kx_ref[lo:hi, :]), NEG_INF)
            m = jnp.max(s_own, axis=-1, keepdims=True)
            if j > 0:
                s_past = _nt(qx, kx_ref[0:lo, :])
                m = jnp.maximum(m, jnp.max(s_past, axis=-1, keepdims=True))
            p_own = jnp.exp(s_own - m)
            l = jnp.sum(p_own, axis=-1, keepdims=True)
            acc = _mm(p_own.astype(BF16), v_ref[0, lo:hi, :])
            if j > 0:
                p_past = jnp.exp(s_past - m)
                l = l + jnp.sum(p_past, axis=-1, keepdims=True)
                acc = acc + _mm(p_past.astype(BF16), v_ref[0, 0:lo, :])
            outs.append(acc / l)
        o_ref[0, lo:hi, :] = jnp.where(lo_half, outs[0], outs[1]).astype(o_ref.dtype)


def _moba_prompt(q, kb, vb, km, tables):
    b, s, _ = q.shape
    nb = s // MOBA_BLOCK
    assert nb == 8, "block-indicator columns are laid out for 8 MoBA blocks"
    npair = N_HEADS_A // 2
    seq_spec = pl.BlockSpec((1, s, LANES), lambda bi, p: (bi, 0, p))
    return pl.pallas_call(
        functools.partial(_moba_prompt_body, nb=nb),
        grid=(b, npair),
        in_specs=[seq_spec, seq_spec, seq_spec,
                  pl.BlockSpec((1, nb, LANES), lambda bi, p: (bi, 0, p)),
                  pl.BlockSpec((1, s, LANES), lambda bi, p: (p, 0, 0))],
        out_specs=seq_spec,
        out_shape=jax.ShapeDtypeStruct((b, s, W_A), BF16),
        scratch_shapes=[pltpu.VMEM((s, LANES), BF16)] * 4,
        compiler_params=_params("parallel", "parallel"),
        name="moba_prompt",
    )(q, kb, vb, km, tables)


def _moba_sample_stages(pt_ref, q_ref, kn_ref, vn_ref, kt_hbm, vt_hbm, o_ref, buf, sem, s_ref,
                        p_ref, *, layer, n_real, past_len):
    b = pl.program_id(0)
    n_seq = pl.num_programs(0)
    n_pages = s_ref.shape[0]
    ring = PAGE_SLOTS
    rows = N_HEADS_A * n_real

    def page_copy(hbm, seq, pg):
        slot = pg % ring
        return pltpu.make_async_copy(hbm.at[layer, pt_ref[seq, pg]], buf.at[slot], sem.at[slot])

    @pl.when(b == 0)
    def _():
        for pg in range(ring):
            page_copy(kt_hbm, b, pg).start()

    rowi = lax.broadcasted_iota(jnp.int32, (rows, W_A), 0)
    lanei = lax.broadcasted_iota(jnp.int32, (rows, W_A), 1)
    head_mask = (lanei >> 6) == (rowi & (N_HEADS_A - 1))
    q = q_ref[0]
    q_rep = jnp.concatenate(
        [jnp.broadcast_to(q[t:t + 1, :], (N_HEADS_A, W_A)) for t in range(n_real)], axis=0)
    q_hi, q_lo = _split_bf16(jnp.where(head_mask, q_rep, 0.0))
    q2 = jnp.concatenate([q_hi, q_lo], axis=0)

    def raw_scores(keys_t=None, keys=None):
        s2 = _mm(q2, keys_t) if keys_t is not None else _nt(q2, keys)
        return s2[:rows] + s2[rows:]

    lane = lax.broadcasted_iota(jnp.int32, (1, LANES), 1)

    def start_group(hbm, seq, first):
        for pg in range(first, first + PAGE_GROUP):
            page_copy(hbm, seq, pg).start()

    def wait_group(hbm, first):
        for pg in range(first, first + PAGE_GROUP):
            page_copy(hbm, b, pg).wait()

    st = {"gsum": jnp.zeros((rows, LANES), F32)}

    def k_group(first):
        gsum = st["gsum"]
        for pg in range(first, first + PAGE_GROUP):
            s = raw_scores(keys_t=buf[pg % ring].astype(BF16))
            s_ref[pg] = s
            gsum = jnp.where(lane == pg, jnp.sum(s, axis=-1, keepdims=True), gsum)
        st["gsum"] = gsum

    def k_refill(first):
        if first + ring < n_pages:
            start_group(kt_hbm, b, first + ring)
        else:
            start_group(vt_hbm, b, first + ring - n_pages)

    def select_and_softmax():
        gsum = st["gsum"]
        gate = (gsum + pltpu.roll(gsum, LANES - 1, 1)) * (1.0 / MOBA_BLOCK)
        valid = ((lane & 1) == 0) & (lane < n_pages)
        sel = _rank_select(gate, lane, valid, range(0, n_pages, 2))

        rcol = lax.broadcasted_iota(jnp.int32, (rows, 1), 0)
        slope = jnp.exp2(-((rcol & (N_HEADS_A - 1)) + 1).astype(F32))
        qpos = past_len + (rcol >> 3)

        m_lanes = jnp.full((rows, LANES), NEG_INF, F32)
        for blk in range(n_pages // 2):
            chosen = sel[:, 2 * blk:2 * blk + 1] > 0.5
            for pg in (2 * blk, 2 * blk + 1):
                kpos = pg * LANES + lane
                s = s_ref[pg] - slope * (qpos - kpos).astype(F32)
                s = jnp.where(chosen, s, NEG_INF)
                s_ref[pg] = s
                m_lanes = jnp.maximum(m_lanes, s)
        tnew = lax.broadcasted_iota(jnp.int32, (1, SAMPLE_ROWS), 1)
        dist = (rcol >> 3) - tnew
        s_own = raw_scores(keys=kn_ref[0]) - slope * dist.astype(F32)
        s_own = jnp.where(dist >= 0, s_own, NEG_INF)
        m = jnp.maximum(jnp.max(m_lanes, axis=-1, keepdims=True),
                        jnp.max(s_own, axis=-1, keepdims=True))

        l_lanes = jnp.zeros((rows, LANES), F32)
        for pg in range(n_pages):
            p = jnp.exp(s_ref[pg] - m)
            l_lanes = l_lanes + p
            p_ref[pg] = p.astype(BF16)
        p_own = jnp.exp(s_own - m)
        st["l"] = jnp.sum(l_lanes, axis=-1, keepdims=True) + jnp.sum(p_own, axis=-1, keepdims=True)
        st["acc"] = _mm(p_own.astype(BF16), vn_ref[0])

    def v_group(first):
        acc = st["acc"]
        for pg in range(first, first + PAGE_GROUP):
            acc = acc + _nt(p_ref[pg], buf[pg % ring].astype(BF16))
        st["acc"] = acc

    def v_refill(first):
        if first + ring < n_pages:
            start_group(vt_hbm, b, first + ring)
        else:
            @pl.when(b + 1 < n_seq)
            def _():
                start_group(kt_hbm, b + 1, first + ring - n_pages)

    def finish():
        o = jnp.where(head_mask, st["acc"] / st["l"], 0.0)
        out_rows = [jnp.sum(o[N_HEADS_A * t:N_HEADS_A * (t + 1), :], axis=0, keepdims=True)
                    for t in range(n_real)]
        out_rows.append(jnp.zeros((SAMPLE_ROWS - n_real, W_A), F32))
        o_ref[0] = jnp.concatenate(out_rows, axis=0).astype(o_ref.dtype)

    part = functools.partial
    nothing = lambda: None
    groups = range(0, n_pages, PAGE_GROUP)
    return ([(part(wait_group, kt_hbm, f), part(k_group, f), part(k_refill, f)) for f in groups]
            + [(nothing, select_and_softmax, nothing)]
            + [(part(wait_group, vt_hbm, f), part(v_group, f), part(v_refill, f)) for f in groups]
            + [(nothing, finish, nothing)])


def _ffn_moba_sample_body(pt_ref, x_ref, g_ref, w1_ref, w3_ref, w2_ref, q_ref, kn_ref, vn_ref,
                          kt_hbm, vt_hbm, y_ref, o_ref, buf, sem, s_ref, p_ref,
                          *, layer, n_real, past_len):
    stages = _moba_sample_stages(pt_ref, q_ref, kn_ref, vn_ref, kt_hbm, vt_hbm, o_ref, buf, sem,
                                 s_ref, p_ref, layer=layer, n_real=n_real, past_len=past_len)
    x = x_ref[...]
    h = _rms(x, g_ref[...]).astype(BF16)
    n_chunks = D_FF // FFN_CHUNK
    assert len(stages) <= n_chunks
    acc = jnp.zeros(x.shape, F32)
    for c in range(n_chunks):
        cols = slice(c * FFN_CHUNK, (c + 1) * FFN_CHUNK)
        a = _mm(h, w1_ref[:, cols])
        gate = _mm(h, w3_ref[:, cols])
        act = (a * jax.nn.sigmoid(a) * gate).astype(BF16)
        acc = acc + _mm(act, w2_ref[cols, :])
        if c < len(stages):
            for part in stages[c]:
                part()
    y_ref[...] = x + 0.5 * acc


def _ffn_moba_sample(x, layer, g, w1, w3, w2, tm, page_table, q, kn, vn, cache_kt, cache_vt, n_real):
    m = x.shape[0]
    bd, n_pages = page_table.shape
    page = cache_kt.shape[-1]
    assert m // tm == bd, "one ffn row tile per sample sequence"
    assert page == LANES and MOBA_BLOCK == 2 * page
    assert PAGE_SLOTS <= n_pages <= LANES and n_pages % PAGE_GROUP == 0 == PAGE_SLOTS % PAGE_GROUP
    rows = N_HEADS_A * n_real

    def layer_spec(shape):
        return pl.BlockSpec((None,) + shape, lambda i, pt: (layer, 0, 0), pipeline_mode=pl.Buffered(1))

    row_spec = pl.BlockSpec((tm, D_MODEL), lambda i, pt: (i, 0))
    tok_spec = pl.BlockSpec((1, SAMPLE_ROWS, W_A), lambda i, pt: (i, 0, 0))
    hbm_spec = pl.BlockSpec(memory_space=pl.ANY)
    grid_spec = pltpu.PrefetchScalarGridSpec(
        num_scalar_prefetch=1,
        grid=(bd,),
        in_specs=[row_spec, layer_spec((1, D_MODEL)), layer_spec((D_MODEL, D_FF)),
                  layer_spec((D_MODEL, D_FF)), layer_spec((D_FF, D_MODEL)),
                  tok_spec, tok_spec, tok_spec, hbm_spec, hbm_spec],
        out_specs=[row_spec, tok_spec],
        scratch_shapes=[pltpu.VMEM((PAGE_SLOTS, W_A, page), F32),
                        pltpu.SemaphoreType.DMA((PAGE_SLOTS,)),
                        pltpu.VMEM((n_pages, rows, LANES), F32),
                        pltpu.VMEM((n_pages, rows, LANES), BF16)],
    )
    return pl.pallas_call(
        functools.partial(_ffn_moba_sample_body, layer=layer, n_real=n_real,
                          past_len=n_pages * page),
        grid_spec=grid_spec,
        out_shape=[jax.ShapeDtypeStruct((m, D_MODEL), F32),
                   jax.ShapeDtypeStruct((bd, SAMPLE_ROWS, W_A), BF16)],
        compiler_params=_params("arbitrary"),
        name="ffn_moba_sample",
    )(page_table, x, g, w1, w3, w2, q, kn, vn, cache_kt, cache_vt)


def _log_decay_rows():
    log_decay = jnp.log(1.0 - 2.0 ** (-5.0 - jnp.arange(N_HEADS_R, dtype=F32)))
    return jnp.broadcast_to(log_decay.reshape(N_HEADS_R, 1, 1), (N_HEADS_R, 1, LANES))


def _retention_body(lg_ref, q_ref, k_ref, v_ref, gr_ref, gain_ref, bias_ref, s0_ref,
                    o_ref, s_out_ref, *, chunk, n_real, n_chunks, heads):
    hd = KEY_DIM_R
    ii = lax.broadcasted_iota(jnp.int32, (chunk, chunk), 0)
    jj = lax.broadcasted_iota(jnp.int32, (chunk, chunk), 1)
    diff = (ii - jj).astype(F32)
    icol = lax.broadcasted_iota(jnp.int32, (chunk, 1), 0).astype(F32)
    for h in range(heads):
        cols = slice(h * hd, (h + 1) * hd)
        lg = lg_ref[h][:, 0:1]
        dmat = jnp.where(diff >= 0, jnp.exp(lg * jnp.maximum(diff, 0.0)), 0.0)
        qdec = jnp.exp(lg * (icol + 1.0))
        kdec = jnp.where(icol < n_real, jnp.exp(lg * (n_real - 1.0 - icol)), 0.0)
        chunk_decay = jnp.exp(lg * float(n_real))
        gain = gain_ref[:, cols]
        bias = bias_ref[:, cols]
        state = s0_ref[0, h]
        for c in range(n_chunks):
            rows = slice(c * chunk, (c + 1) * chunk)
            qc = q_ref[0, rows, cols]
            kc = k_ref[0, rows, cols]
            vc = v_ref[0, rows, cols]
            inner = _nt(qc, kc) * dmat
            o = _mm(inner.astype(BF16), vc) + _mm(qc, state.astype(BF16)) * qdec
            kd = (kc.astype(F32) * kdec).astype(BF16)
            state = chunk_decay * state + _tn(kd, vc)
            mu = jnp.mean(o, axis=-1, keepdims=True)
            d = o - mu
            var = jnp.mean(d * d, axis=-1, keepdims=True)
            y = d * lax.rsqrt(var + EPS) * gain + bias
            gr = gr_ref[0, rows, cols]
            o_ref[0, rows, cols] = (y * (gr * jax.nn.sigmoid(gr))).astype(o_ref.dtype)
        s_out_ref[0, h] = state


def _retention(lg, q, k, v, gr, gain, bias, layer, state0, state_offset, chunk, n_real, heads):
    b, s, _ = q.shape
    hd = KEY_DIM_R
    width = heads * hd
    seq_spec = pl.BlockSpec((1, s, width), lambda bi, g: (bi, 0, g))
    vec_spec = pl.BlockSpec((None, 1, width), lambda bi, g: (layer, 0, g))
    st_spec = pl.BlockSpec((1, heads, hd, hd), lambda bi, g: (bi, g, 0, 0))
    st_in_spec = pl.BlockSpec((1, heads, hd, hd), lambda bi, g: (state_offset + bi, g, 0, 0))
    return pl.pallas_call(
        functools.partial(_retention_body, chunk=chunk, n_real=n_real, n_chunks=s // chunk,
                          heads=heads),
        grid=(b, N_HEADS_R // heads),
        in_specs=[pl.BlockSpec((heads, 1, LANES), lambda bi, g: (g, 0, 0)),
                  seq_spec, seq_spec, seq_spec, seq_spec, vec_spec, vec_spec, st_in_spec],
        out_specs=[seq_spec, st_spec],
        out_shape=[jax.ShapeDtypeStruct((b, s, W_R), BF16),
                   jax.ShapeDtypeStruct((b, N_HEADS_R, hd, hd), F32)],
        compiler_params=_params("parallel", "parallel"),
        name="retention",
    )(lg, q, k, v, gr, gain, bias, state0)


def kernel(x_prompt, x_sample, cache_k, cache_v, state_ret, page_table, g_ffn1, w1_ffn1, w3_ffn1,
           w2_ffn1, g_mix, w_in, g_q, g_k, gn_gain, gn_bias, w_pa, w_pr, w_o, g_ffn2, w1_ffn2,
           w3_ffn2, w2_ffn2):
    b, s, _ = x_prompt.shape
    bd, sd, _ = x_sample.shape
    depth = w_in.shape[0]
    n_pool, page = cache_k.shape[1], cache_k.shape[2]
    mp, ms = b * s, bd * SAMPLE_ROWS
    tm_p, tm_s = 512, ms

    xp = x_prompt.reshape(mp, D_MODEL)
    xs = jnp.pad(x_sample, ((0, 0), (0, SAMPLE_ROWS - sd), (0, 0))).reshape(ms, D_MODEL)

    cache_kt = jnp.transpose(cache_k, (0, 1, 3, 4, 2)).reshape(depth, n_pool, W_A, page)
    cache_vt = jnp.transpose(cache_v, (0, 1, 3, 4, 2)).reshape(depth, n_pool, W_A, page)

    tables = _moba_tables(s)
    lg = _log_decay_rows()
    head_ones = jnp.kron(jnp.eye(N_HEADS_A, dtype=F32),
                         jnp.ones((HEAD_DIM_A, HEAD_DIM_A), F32)).astype(BF16)
    zero_state = jnp.zeros((b, N_HEADS_R, KEY_DIM_R, VAL_DIM_R), F32)
    states_in = state_ret.reshape(depth * bd, N_HEADS_R, KEY_DIM_R, VAL_DIM_R)

    vec = lambda a: a.reshape(depth, 1, -1)
    g1, gm, g2, gain, bias = vec(g_ffn1), vec(g_mix), vec(g_ffn2), vec(gn_gain), vec(gn_bias)
    gq, gk = vec(jnp.tile(g_q, (1, N_HEADS_A))), vec(jnp.tile(g_k, (1, N_HEADS_A)))
    w1a, w3a, w2a = w1_ffn1.astype(BF16), w3_ffn1.astype(BF16), w2_ffn1.astype(BF16)
    w1b, w3b, w2b = w1_ffn2.astype(BF16), w3_ffn2.astype(BF16), w2_ffn2.astype(BF16)
    win, wpa, wpr, wo = w_in.astype(BF16), w_pa.astype(BF16), w_pr.astype(BF16), w_o.astype(BF16)

    kv_p = kv_s = None
    sp_l, ss_l = [], []
    for l in range(depth):
        xp = _ffn(xp, l, g1, w1a, w3a, w2a, tm_p)
        q, kf, kb, vf, vb, qr, kr, vr, gr, ga, gb, km = _inproj(
            xp, l, depth, gm, win, gq, gk, head_ones, tm_p, kv_p, seq_len=s)
        kv_p = (kf, vf)
        r3 = lambda a: a.reshape(b, s, a.shape[-1])
        oa = _moba_prompt(r3(q), r3(kb), r3(vb), km.reshape(b, s // MOBA_BLOCK, W_A), tables)
        o_r, st = _retention(lg, r3(qr), r3(kr), r3(vr), r3(gr), gain, bias, l, zero_state, 0,
                             RET_CHUNK_PROMPT, RET_CHUNK_PROMPT, RET_HEADS_PROMPT)
        xp = _outproj(oa.reshape(mp, W_A), o_r.reshape(mp, W_R), ga, gb, xp, l, wpa, wpr, wo, tm_p)
        sp_l.append(st)

        xs = _ffn(xs, l, g1, w1a, w3a, w2a, tm_s)
        q, kf, kb, vf, vb, qr, kr, vr, gr, ga, gb = _inproj(
            xs, l, depth, gm, win, gq, gk, head_ones, tm_s, kv_s)
        kv_s = (kf, vf)
        r3 = lambda a: a.reshape(bd, SAMPLE_ROWS, a.shape[-1])
        xp, oa = _ffn_moba_sample(xp, l, g2, w1b, w3b, w2b, tm_p, page_table, r3(q), r3(kb), r3(vb),
                                  cache_kt, cache_vt, sd)
        o_r, st = _retention(lg, r3(qr), r3(kr), r3(vr), r3(gr), gain, bias, l, states_in, l * bd,
                             SAMPLE_ROWS, sd, N_HEADS_R)
        xs = _outproj(oa.reshape(ms, W_A), o_r.reshape(ms, W_R), ga, gb, xs, l, wpa, wpr, wo, tm_s)
        xs = _ffn(xs, l, g2, w1b, w3b, w2b, tm_s)
        ss_l.append(st)

    y_prompt = xp.reshape(b, s, D_MODEL)
    y_sample = xs.reshape(bd, SAMPLE_ROWS, D_MODEL)[:, :sd]
    seq_minor = lambda a: a.reshape(depth, b, N_HEADS_A, HEAD_DIM_A, s).transpose(0, 1, 4, 2, 3)
    k_prompt, v_prompt = seq_minor(kv_p[0]), seq_minor(kv_p[1])
    heads5 = lambda a: a.reshape(depth, bd, SAMPLE_ROWS, N_HEADS_A, HEAD_DIM_A)[:, :, :sd]
    k_sample, v_sample = heads5(kv_s[0]), heads5(kv_s[1])
    return (y_prompt, y_sample, k_prompt, v_prompt, jnp.stack(sp_l),
            k_sample, v_sample, jnp.stack(ss_l))
```

```python
import functools

import jax
import jax.numpy as jnp
from jax import lax
from jax.experimental import pallas as pl
from jax.experimental.pallas import tpu as pltpu

F32 = jnp.float32
BF16 = jnp.bfloat16

D_MODEL = 1024
D_FF = 2816
N_HEADS_A = 8
HEAD_DIM_A = 64
MOBA_BLOCK = 256
MOBA_TOPK = 3
N_HEADS_R = 4
KEY_DIM_R = 128
VAL_DIM_R = 128
EPS = 1e-6
NEG_INF = -1e30
W_A = N_HEADS_A * HEAD_DIM_A
W_R = N_HEADS_R * KEY_DIM_R

LANES = 128
VMEM_LIMIT_BYTES = 56 * 1024 * 1024
SAMPLE_ROWS = 16
RET_CHUNK_PROMPT = 256
RET_HEADS_PROMPT = 2
PAGE_SLOTS = 32
PAGE_GROUP = 16
FFN_CHUNK = 256


def _nt(a, b):
    return lax.dot_general(a, b, (((1,), (1,)), ((), ())), preferred_element_type=F32)


def _tn(a, b):
    return lax.dot_general(a, b, (((0,), (0,)), ((), ())), preferred_element_type=F32)


def _mm(a, b):
    return jnp.dot(a, b, preferred_element_type=F32)


def _rms(x, g):
    return x * lax.rsqrt(jnp.mean(x * x, axis=-1, keepdims=True) + EPS) * g


def _split_bf16(x):
    hi = x.astype(BF16)
    lo = (x - hi.astype(F32)).astype(BF16)
    return hi, lo


def _params(*sem):
    return pltpu.CompilerParams(dimension_semantics=sem, vmem_limit_bytes=VMEM_LIMIT_BYTES)


def _row_spec(tm, cols):
    return pl.BlockSpec((tm, cols), lambda i: (i, 0))


def _const_spec(shape):
    return pl.BlockSpec(shape, lambda i: (0,) * len(shape), pipeline_mode=pl.Buffered(1))


def _layer_spec(layer, shape):
    return pl.BlockSpec((None,) + shape, lambda i: (layer, 0, 0), pipeline_mode=pl.Buffered(1))


def _ffn_body(x_ref, g_ref, w1_ref, w3_ref, w2_ref, o_ref):
    x = x_ref[...]
    h = _rms(x, g_ref[...]).astype(BF16)
    a = _mm(h, w1_ref[...])
    b = _mm(h, w3_ref[...])
    act = (a * jax.nn.sigmoid(a) * b).astype(BF16)
    o_ref[...] = x + 0.5 * _mm(act, w2_ref[...])


def _ffn(x, layer, g, w1, w3, w2, tm):
    m = x.shape[0]
    return pl.pallas_call(
        _ffn_body,
        grid=(m // tm,),
        in_specs=[_row_spec(tm, D_MODEL), _layer_spec(layer, (1, D_MODEL)),
                  _layer_spec(layer, (D_MODEL, D_FF)), _layer_spec(layer, (D_MODEL, D_FF)),
                  _layer_spec(layer, (D_FF, D_MODEL))],
        out_specs=_row_spec(tm, D_MODEL),
        out_shape=jax.ShapeDtypeStruct((m, D_MODEL), F32),
        compiler_params=_params("parallel"),
        name="ffn",
    )(x, g, w1, w3, w2)


def _ffn_few_rows_body(x_ref, g_ref, w1_ref, w3_ref, w2_ref, o_ref, h_ref, acc_ref):
    c = pl.program_id(0)

    @pl.when(c == 0)
    def _():
        h_ref[...] = _rms(x_ref[...], g_ref[...]).astype(BF16)
        acc_ref[...] = jnp.zeros_like(acc_ref)

    h = h_ref[...]
    a = _mm(h, w1_ref[...])
    act = (a * jax.nn.sigmoid(a) * _mm(h, w3_ref[...])).astype(BF16)
    acc_ref[...] += _mm(act, w2_ref[...])

    @pl.when(c == pl.num_programs(0) - 1)
    def _():
        o_ref[...] = x_ref[...] + 0.5 * acc_ref[...]


def _ffn_few_rows(x, layer, g, w1, w3, w2):
    m = x.shape[0]
    whole = pl.BlockSpec((m, D_MODEL), lambda c: (0, 0))
    return pl.pallas_call(
        _ffn_few_rows_body,
        grid=(D_FF // FFN_CHUNK,),
        in_specs=[whole, pl.BlockSpec((None, 1, D_MODEL), lambda c: (layer, 0, 0)),
                  pl.BlockSpec((None, D_MODEL, FFN_CHUNK), lambda c: (layer, 0, c)),
                  pl.BlockSpec((None, D_MODEL, FFN_CHUNK), lambda c: (layer, 0, c)),
                  pl.BlockSpec((None, FFN_CHUNK, D_MODEL), lambda c: (layer, c, 0))],
        out_specs=whole,
        out_shape=jax.ShapeDtypeStruct((m, D_MODEL), F32),
        scratch_shapes=[pltpu.VMEM((m, D_MODEL), BF16), pltpu.VMEM((m, D_MODEL), F32)],
        compiler_params=_params("arbitrary"),
        name="ffn_few_rows",
    )(x, g, w1, w3, w2)


_IN_COLS = (W_A, W_A, W_A, W_R, W_R, W_R, W_R, D_MODEL, D_MODEL)
_IN_OFFS = tuple(sum(_IN_COLS[:i]) for i in range(len(_IN_COLS) + 1))
D_IN = _IN_OFFS[-1]


def _inproj_body(x_ref, g_ref, w_ref, gq_ref, gk_ref, e_ref, *rest, seq_major):
    n_out = 12 if seq_major else 11
    q_o, kf_o, kb_o, vf_o, vb_o, qr_o, kr_o, vr_o, gr_o, ga_o, gb_o = rest[-n_out:][:11]
    h = _rms(x_ref[...], g_ref[...]).astype(BF16)

    def proj(i):
        return _mm(h, w_ref[:, _IN_OFFS[i]:_IN_OFFS[i + 1]])

    def head_rms(z, gain):
        ss = _mm((z * z).astype(BF16), e_ref[...])
        return z * lax.rsqrt(ss * (1.0 / HEAD_DIM_A) + EPS) * gain

    q_o[...] = head_rms(proj(0), gq_ref[...]) * (HEAD_DIM_A ** -0.5)
    k = head_rms(proj(1), gk_ref[...])
    v = proj(2)
    vb_o[...] = v.astype(BF16)
    if seq_major:
        kt = k.T
        kf_o[...] = kt
        kb_o[...] = kt.astype(BF16)
        vf_o[...] = v.T
        km_o = rest[-1]
        for r in range(km_o.shape[0]):
            km_o[r] = jnp.mean(k[r * MOBA_BLOCK:(r + 1) * MOBA_BLOCK, :], axis=0, keepdims=True)
    else:
        kf_o[...] = k
        kb_o[...] = k.astype(BF16)
        vf_o[...] = v
    qr_o[...] = proj(3).astype(BF16)
    kr_o[...] = (proj(4) * (KEY_DIM_R ** -0.5)).astype(BF16)
    vr_o[...] = proj(5).astype(BF16)
    gr_o[...] = proj(6)
    ga_o[...] = proj(7)
    gb_o[...] = proj(8)


def _inproj(x, layer, depth, g, w, gq, gk, e, tm, kv_stacks=None, seq_len=None):
    m = x.shape[0]
    outs = [(W_A, F32), (W_A, F32), (W_A, BF16), (W_A, F32), (W_A, BF16),
            (W_R, BF16), (W_R, BF16), (W_R, BF16), (W_R, F32), (D_MODEL, F32), (D_MODEL, F32)]
    stacked = (1, 3)
    out_specs = [_row_spec(tm, c) for c, _ in outs]
    out_shape = [jax.ShapeDtypeStruct((m, c), dt) for c, dt in outs]
    for n in stacked:
        if seq_len is None:
            out_specs[n] = pl.BlockSpec((None, tm, W_A), lambda i: (layer, i, 0))
            out_shape[n] = jax.ShapeDtypeStruct((depth, m, W_A), F32)
        else:
            tiles = seq_len // tm
            out_specs[n] = pl.BlockSpec((None, None, W_A, tm),
                                        lambda i: (layer, i // tiles, 0, i % tiles))
            out_shape[n] = jax.ShapeDtypeStruct((depth, m // seq_len, W_A, seq_len), F32)
    if seq_len is not None:
        out_specs[2] = pl.BlockSpec((None, W_A, tm), lambda i: (i // tiles, 0, i % tiles))
        out_shape[2] = jax.ShapeDtypeStruct((m // seq_len, W_A, seq_len), BF16)
        out_specs.append(pl.BlockSpec((tm // MOBA_BLOCK, 1, W_A), lambda i: (i, 0, 0)))
        out_shape.append(jax.ShapeDtypeStruct((m // MOBA_BLOCK, 1, W_A), F32))
    in_specs = [_row_spec(tm, D_MODEL), _layer_spec(layer, (1, D_MODEL)),
                _layer_spec(layer, (D_MODEL, D_IN)), _layer_spec(layer, (1, W_A)),
                _layer_spec(layer, (1, W_A)), _const_spec((W_A, W_A))]
    args = [x, g, w, gq, gk, e]
    aliases = {}
    if kv_stacks is not None:
        for n, stack in zip(stacked, kv_stacks):
            aliases[len(args)] = n
            in_specs.append(pl.BlockSpec(memory_space=pl.ANY))
            args.append(stack)
    return pl.pallas_call(
        functools.partial(_inproj_body, seq_major=seq_len is not None),
        grid=(m // tm,),
        in_specs=in_specs,
        out_specs=out_specs,
        out_shape=out_shape,
        input_output_aliases=aliases,
        compiler_params=_params("parallel"),
        name="inproj",
    )(*args)


def _outproj_body(oa_ref, or_ref, ga_ref, gb_ref, x_ref, wpa_ref, wpr_ref, wo_ref, o_ref):
    a = _mm(oa_ref[...], wpa_ref[...])
    r = _mm(or_ref[...], wpr_ref[...])
    merged = jax.nn.sigmoid(ga_ref[...]) * a + jax.nn.sigmoid(gb_ref[...]) * r
    o_ref[...] = x_ref[...] + _mm(merged.astype(BF16), wo_ref[...])


def _outproj(oa, o_r, ga, gb, x, layer, wpa, wpr, wo, tm):
    m = x.shape[0]
    return pl.pallas_call(
        _outproj_body,
        grid=(m // tm,),
        in_specs=[_row_spec(tm, W_A), _row_spec(tm, W_R), _row_spec(tm, D_MODEL),
                  _row_spec(tm, D_MODEL), _row_spec(tm, D_MODEL),
                  _layer_spec(layer, (W_A, D_MODEL)), _layer_spec(layer, (W_R, D_MODEL)),
                  _layer_spec(layer, (D_MODEL, D_MODEL))],
        out_specs=_row_spec(tm, D_MODEL),
        out_shape=jax.ShapeDtypeStruct((m, D_MODEL), F32),
        compiler_params=_params("parallel"),
        name="outproj",
    )(oa, o_r, ga, gb, x, wpa, wpr, wo)


def _rank_select(gate, lane, n_valid_mask, candidates):
    gate = jnp.where(n_valid_mask, gate, NEG_INF)
    cnt = jnp.zeros(gate.shape, F32)
    for c in candidates:
        gc = gate[:, c:c + 1]
        beats = (gc > gate) | ((gc == gate) & (lane > c))
        cnt = cnt + jnp.where(beats, 1.0, 0.0)
    return jnp.where((cnt < MOBA_TOPK) & n_valid_mask, 1.0, 0.0)


def _moba_tables(s):
    nb = s // MOBA_BLOCK
    pos = jnp.arange(s)
    blk = (pos // MOBA_BLOCK).astype(F32)
    within = (pos % MOBA_BLOCK).astype(F32)
    ind = (pos[:, None] // MOBA_BLOCK == jnp.arange(nb)[None, :]).astype(F32)
    slopes = 2.0 ** (-8.0 * (jnp.arange(N_HEADS_A, dtype=F32) + 1.0) / N_HEADS_A)

    def half(slope):
        cols = [ind, (slope * MOBA_BLOCK * blk)[:, None], (slope * within)[:, None],
                jnp.zeros((s, HEAD_DIM_A - nb - 2), F32)]
        return jnp.concatenate(cols, axis=1)

    pairs = [jnp.concatenate([half(slopes[2 * p + 1]), half(slopes[2 * p])], axis=1).T
             for p in range(N_HEADS_A // 2)]
    return jnp.stack(pairs).astype(BF16)


def _moba_prompt_body(q_ref, k_ref, v_ref, km_ref, tab_ref, o_ref, qa_ref, qb_ref, ka_ref, kb_ref,
                      va_ref, vb_ref, *, nb):
    blk = MOBA_BLOCK
    s_len = nb * blk
    q2 = q_ref[0]
    lane = lax.broadcasted_iota(jnp.int32, (1, LANES), 1)
    lo_half = lane < HEAD_DIM_A

    km = km_ref[0]
    km_parts = []
    for part in (jnp.where(lo_half, km, 0.0), jnp.where(lo_half, 0.0, km)):
        km_parts.extend(_split_bf16(part))
    kms = jnp.concatenate(km_parts, axis=0)
    q_hi, q_lo = _split_bf16(q2)
    r_hi = _nt(kms, q_hi)
    r_lo = _nt(kms, q_lo)
    gates = [r_hi[0:nb] + r_hi[nb:2 * nb] + r_lo[0:nb],
             r_hi[2 * nb:3 * nb] + r_hi[3 * nb:4 * nb] + r_lo[2 * nb:3 * nb]]

    brow = lax.broadcasted_iota(jnp.int32, (nb, s_len), 0)
    own = lax.broadcasted_iota(jnp.int32, (nb, s_len), 1) // blk
    past = brow < own

    def block_bias(gate):
        g = jnp.where(past, gate, NEG_INF)
        cnt = jnp.zeros((nb, s_len), F32)
        for c in range(nb):
            gc = g[c:c + 1, :]
            beats = (gc > g) | ((gc == g) & (brow > c))
            cnt = cnt + jnp.where(beats, 1.0, 0.0)
        allowed = ((cnt < MOBA_TOPK) & past) | (brow == own)
        return jnp.where(allowed, 0.0, NEG_INF)

    ones2 = jnp.where(lax.broadcasted_iota(jnp.int32, (8, s_len), 0) < 2, 1.0, 0.0)
    pad = jnp.zeros((HEAD_DIM_A - nb - 8, s_len), F32)
    ext_t = jnp.concatenate([block_bias(gates[1]), ones2, pad, block_bias(gates[0]), ones2, pad], axis=0)
    ext = ext_t.T
    qa_ref[...] = jnp.where(lo_half, q2, ext).astype(BF16)
    qb_ref[...] = jnp.where(lo_half, ext, q2).astype(BF16)
    k2 = k_ref[0]
    tab = tab_ref[0]
    lo_rows = lax.broadcasted_iota(jnp.int32, (LANES, 1), 0) < HEAD_DIM_A
    ka_ref[...] = jnp.where(lo_rows, k2, tab)
    kb_ref[...] = jnp.where(lo_rows, tab, k2)
    v2 = v_ref[0]
    sum_lane_a, sum_lane_b = HEAD_DIM_A, 0
    va_ref[...] = jnp.where(lo_half, v2, jnp.where(lane == sum_lane_a, 1.0, 0.0).astype(BF16))
    vb_ref[...] = jnp.where(lo_half, jnp.where(lane == sum_lane_b, 1.0, 0.0).astype(BF16), v2)

    row = lax.broadcasted_iota(jnp.int32, (blk, blk), 0)
    col = lax.broadcasted_iota(jnp.int32, (blk, blk), 1)
    causal = col <= row

    for j in range(nb):
        lo, hi = j * blk, (j + 1) * blk
        outs = []
        for qx_ref, kx_ref, vx_ref, sum_lane in ((qa_ref, ka_ref, va_ref, sum_lane_a),
                                                 (qb_ref, kb_ref, vb_ref, sum_lane_b)):
            qx = qx_ref[lo:hi, :]
            s_own = jnp.where(causal, _mm(qx, kx_ref[:, lo:hi]), NEG_INF)
            m = jnp.max(s_own, axis=-1, keepdims=True)
            if j > 0:
                s_past = _mm(qx, kx_ref[:, 0:lo])
                m = jnp.maximum(m, jnp.max(s_past, axis=-1, keepdims=True))
            acc = _mm(jnp.exp(s_own - m).astype(BF16), vx_ref[lo:hi, :])
            if j > 0:
                acc = acc + _mm(jnp.exp(s_past - m).astype(BF16), vx_ref[0:lo, :])
            outs.append(acc / acc[:, sum_lane:sum_lane + 1])
        o_ref[0, lo:hi, :] = jnp.where(lo_half, outs[0], outs[1]).astype(o_ref.dtype)


def _moba_prompt(q, kb, vb, km, tables):
    b, s, _ = q.shape
    nb = s // MOBA_BLOCK
    assert nb == 8, "block-indicator columns are laid out for 8 MoBA blocks"
    npair = N_HEADS_A // 2
    seq_spec = pl.BlockSpec((1, s, LANES), lambda bi, p: (bi, 0, p))
    return pl.pallas_call(
        functools.partial(_moba_prompt_body, nb=nb),
        grid=(b, npair),
        in_specs=[seq_spec, pl.BlockSpec((1, LANES, s), lambda bi, p: (bi, p, 0)), seq_spec,
                  pl.BlockSpec((1, nb, LANES), lambda bi, p: (bi, 0, p)),
                  pl.BlockSpec((1, LANES, s), lambda bi, p: (p, 0, 0))],
        out_specs=seq_spec,
        out_shape=jax.ShapeDtypeStruct((b, s, W_A), BF16),
        scratch_shapes=[pltpu.VMEM((s, LANES), BF16)] * 2 + [pltpu.VMEM((LANES, s), BF16)] * 2
        + [pltpu.VMEM((s, LANES), BF16)] * 2,
        compiler_params=_params("parallel", "parallel"),
        name="moba_prompt",
    )(q, kb, vb, km, tables)


def _moba_sample_stages(pt_ref, q_ref, kn_ref, vn_ref, kt_hbm, vt_hbm, o_ref, buf, sem, s_ref,
                        p_ref, *, layer, n_real, past_len):
    b = pl.program_id(0)
    n_seq = pl.num_programs(0)
    n_pages = s_ref.shape[0]
    ring = PAGE_SLOTS
    rows = N_HEADS_A * n_real

    def page_copy(hbm, seq, pg):
        slot = pg % ring
        return pltpu.make_async_copy(hbm.at[layer, pt_ref[seq, pg]], buf.at[slot], sem.at[slot])

    @pl.when(b == 0)
    def _():
        for pg in range(ring):
            page_copy(kt_hbm, b, pg).start()

    rowi = lax.broadcasted_iota(jnp.int32, (rows, W_A), 0)
    lanei = lax.broadcasted_iota(jnp.int32, (rows, W_A), 1)
    head_mask = (lanei >> 6) == (rowi & (N_HEADS_A - 1))
    q = q_ref[0]
    q_rep = jnp.concatenate(
        [jnp.broadcast_to(q[t:t + 1, :], (N_HEADS_A, W_A)) for t in range(n_real)], axis=0)
    q_hi, q_lo = _split_bf16(jnp.where(head_mask, q_rep, 0.0))
    q2 = jnp.concatenate([q_hi, q_lo], axis=0)

    def raw_scores(keys_t=None, keys=None):
        s2 = _mm(q2, keys_t) if keys_t is not None else _nt(q2, keys)
        return s2[:rows] + s2[rows:]

    lane = lax.broadcasted_iota(jnp.int32, (1, LANES), 1)

    def start_group(hbm, seq, first):
        for pg in range(first, first + PAGE_GROUP):
            page_copy(hbm, seq, pg).start()

    def wait_group(hbm, first):
        for pg in range(first, first + PAGE_GROUP):
            page_copy(hbm, b, pg).wait()

    st = {"gsum": jnp.zeros((rows, LANES), F32)}

    def k_group(first):
        gsum = st["gsum"]
        for pg in range(first, first + PAGE_GROUP):
            s = raw_scores(keys_t=buf[pg % ring].astype(BF16))
            s_ref[pg] = s
            gsum = jnp.where(lane == pg, jnp.sum(s, axis=-1, keepdims=True), gsum)
        st["gsum"] = gsum

    def k_refill(first):
        if first + ring < n_pages:
            start_group(kt_hbm, b, first + ring)
        else:
            start_group(vt_hbm, b, first + ring - n_pages)

    def select_and_softmax():
        gsum = st["gsum"]
        gate = (gsum + pltpu.roll(gsum, LANES - 1, 1)) * (1.0 / MOBA_BLOCK)
        valid = ((lane & 1) == 0) & (lane < n_pages)
        sel = _rank_select(gate, lane, valid, range(0, n_pages, 2))

        rcol = lax.broadcasted_iota(jnp.int32, (rows, 1), 0)
        slope = jnp.exp2(-((rcol & (N_HEADS_A - 1)) + 1).astype(F32))
        qpos = past_len + (rcol >> 3)

        m_lanes = jnp.full((rows, LANES), NEG_INF, F32)
        for blk in range(n_pages // 2):
            chosen = sel[:, 2 * blk:2 * blk + 1] > 0.5
            for pg in (2 * blk, 2 * blk + 1):
                kpos = pg * LANES + lane
                s = s_ref[pg] - slope * (qpos - kpos).astype(F32)
                s = jnp.where(chosen, s, NEG_INF)
                s_ref[pg] = s
                m_lanes = jnp.maximum(m_lanes, s)
        tnew = lax.broadcasted_iota(jnp.int32, (1, SAMPLE_ROWS), 1)
        dist = (rcol >> 3) - tnew
        s_own = raw_scores(keys=kn_ref[0]) - slope * dist.astype(F32)
        s_own = jnp.where(dist >= 0, s_own, NEG_INF)
        m = jnp.maximum(jnp.max(m_lanes, axis=-1, keepdims=True),
                        jnp.max(s_own, axis=-1, keepdims=True))

        l_lanes = jnp.zeros((rows, LANES), F32)
        for pg in range(n_pages):
            p = jnp.exp(s_ref[pg] - m)
            l_lanes = l_lanes + p
            p_ref[pg] = p.astype(BF16)
        p_own = jnp.exp(s_own - m)
        st["l"] = jnp.sum(l_lanes, axis=-1, keepdims=True) + jnp.sum(p_own, axis=-1, keepdims=True)
        st["acc"] = _mm(p_own.astype(BF16), vn_ref[0])

    def v_group(first):
        acc = st["acc"]
        for pg in range(first, first + PAGE_GROUP):
            acc = acc + _nt(p_ref[pg], buf[pg % ring].astype(BF16))
        st["acc"] = acc

    def v_refill(first):
        if first + ring < n_pages:
            start_group(vt_hbm, b, first + ring)
        else:
            @pl.when(b + 1 < n_seq)
            def _():
                start_group(kt_hbm, b + 1, first + ring - n_pages)

    def finish():
        o = jnp.where(head_mask, st["acc"] / st["l"], 0.0)
        out_rows = [jnp.sum(o[N_HEADS_A * t:N_HEADS_A * (t + 1), :], axis=0, keepdims=True)
                    for t in range(n_real)]
        out_rows.append(jnp.zeros((SAMPLE_ROWS - n_real, W_A), F32))
        o_ref[0] = jnp.concatenate(out_rows, axis=0).astype(o_ref.dtype)

    part = functools.partial
    nothing = lambda: None
    groups = range(0, n_pages, PAGE_GROUP)
    return ([(part(wait_group, kt_hbm, f), part(k_group, f), part(k_refill, f)) for f in groups]
            + [(nothing, select_and_softmax, nothing)]
            + [(part(wait_group, vt_hbm, f), part(v_group, f), part(v_refill, f)) for f in groups]
            + [(nothing, finish, nothing)])


def _ffn_moba_sample_body(pt_ref, x_ref, g_ref, w1_ref, w3_ref, w2_ref, q_ref, kn_ref, vn_ref,
                          kt_hbm, vt_hbm, y_ref, o_ref, buf, sem, s_ref, p_ref,
                          *, layer, n_real, past_len):
    stages = _moba_sample_stages(pt_ref, q_ref, kn_ref, vn_ref, kt_hbm, vt_hbm, o_ref, buf, sem,
                                 s_ref, p_ref, layer=layer, n_real=n_real, past_len=past_len)
    x = x_ref[...]
    h = _rms(x, g_ref[...]).astype(BF16)
    n_chunks = D_FF // FFN_CHUNK
    assert len(stages) <= n_chunks
    acc = jnp.zeros(x.shape, F32)
    for c in range(n_chunks):
        cols = slice(c * FFN_CHUNK, (c + 1) * FFN_CHUNK)
        a = _mm(h, w1_ref[:, cols])
        gate = _mm(h, w3_ref[:, cols])
        act = (a * jax.nn.sigmoid(a) * gate).astype(BF16)
        acc = acc + _mm(act, w2_ref[cols, :])
        if c < len(stages):
            for part in stages[c]:
                part()
    y_ref[...] = x + 0.5 * acc


def _ffn_moba_sample(x, layer, g, w1, w3, w2, tm, page_table, q, kn, vn, cache_kt, cache_vt, n_real):
    m = x.shape[0]
    bd, n_pages = page_table.shape
    page = cache_kt.shape[-1]
    assert m // tm == bd, "one ffn row tile per sample sequence"
    assert page == LANES and MOBA_BLOCK == 2 * page
    assert PAGE_SLOTS <= n_pages <= LANES and n_pages % PAGE_GROUP == 0 == PAGE_SLOTS % PAGE_GROUP
    rows = N_HEADS_A * n_real

    def layer_spec(shape):
        return pl.BlockSpec((None,) + shape, lambda i, pt: (layer, 0, 0), pipeline_mode=pl.Buffered(1))

    row_spec = pl.BlockSpec((tm, D_MODEL), lambda i, pt: (i, 0))
    tok_spec = pl.BlockSpec((1, SAMPLE_ROWS, W_A), lambda i, pt: (i, 0, 0))
    hbm_spec = pl.BlockSpec(memory_space=pl.ANY)
    grid_spec = pltpu.PrefetchScalarGridSpec(
        num_scalar_prefetch=1,
        grid=(bd,),
        in_specs=[row_spec, layer_spec((1, D_MODEL)), layer_spec((D_MODEL, D_FF)),
                  layer_spec((D_MODEL, D_FF)), layer_spec((D_FF, D_MODEL)),
                  tok_spec, tok_spec, tok_spec, hbm_spec, hbm_spec],
        out_specs=[row_spec, tok_spec],
        scratch_shapes=[pltpu.VMEM((PAGE_SLOTS, W_A, page), F32),
                        pltpu.SemaphoreType.DMA((PAGE_SLOTS,)),
                        pltpu.VMEM((n_pages, rows, LANES), F32),
                        pltpu.VMEM((n_pages, rows, LANES), BF16)],
    )
    return pl.pallas_call(
        functools.partial(_ffn_moba_sample_body, layer=layer, n_real=n_real,
                          past_len=n_pages * page),
        grid_spec=grid_spec,
        out_shape=[jax.ShapeDtypeStruct((m, D_MODEL), F32),
                   jax.ShapeDtypeStruct((bd, SAMPLE_ROWS, W_A), BF16)],
        compiler_params=_params("arbitrary"),
        name="ffn_moba_sample",
    )(page_table, x, g, w1, w3, w2, q, kn, vn, cache_kt, cache_vt)


def _log_decay_rows():
    log_decay = jnp.log(1.0 - 2.0 ** (-5.0 - jnp.arange(N_HEADS_R, dtype=F32)))
    return jnp.broadcast_to(log_decay.reshape(N_HEADS_R, 1, 1), (N_HEADS_R, 1, LANES))


def _retention_body(lg_ref, q_ref, k_ref, v_ref, gr_ref, gain_ref, bias_ref, s0_ref,
                    o_ref, s_out_ref, *, chunk, n_real, n_chunks, heads):
    hd = KEY_DIM_R
    ii = lax.broadcasted_iota(jnp.int32, (chunk, chunk), 0)
    jj = lax.broadcasted_iota(jnp.int32, (chunk, chunk), 1)
    diff = (ii - jj).astype(F32)
    icol = lax.broadcasted_iota(jnp.int32, (chunk, 1), 0).astype(F32)
    for h in range(heads):
        cols = slice(h * hd, (h + 1) * hd)
        lg = lg_ref[h][:, 0:1]
        dmat = jnp.where(diff >= 0, jnp.exp(lg * jnp.maximum(diff, 0.0)), 0.0)
        qdec = jnp.exp(lg * (icol + 1.0))
        kdec = jnp.where(icol < n_real, jnp.exp(lg * (n_real - 1.0 - icol)), 0.0)
        chunk_decay = jnp.exp(lg * float(n_real))
        gain = gain_ref[:, cols]
        bias = bias_ref[:, cols]
        state = s0_ref[0, h]
        for c in range(n_chunks):
            rows = slice(c * chunk, (c + 1) * chunk)
            qc = q_ref[0, rows, cols]
            kc = k_ref[0, rows, cols]
            vc = v_ref[0, rows, cols]
            inner = _nt(qc, kc) * dmat
            o = _mm(inner.astype(BF16), vc) + _mm(qc, state.astype(BF16)) * qdec
            kd = (kc.astype(F32) * kdec).astype(BF16)
            state = chunk_decay * state + _tn(kd, vc)
            mu = jnp.mean(o, axis=-1, keepdims=True)
            d = o - mu
            var = jnp.mean(d * d, axis=-1, keepdims=True)
            y = d * lax.rsqrt(var + EPS) * gain + bias
            gr = gr_ref[0, rows, cols]
            o_ref[0, rows, cols] = (y * (gr * jax.nn.sigmoid(gr))).astype(o_ref.dtype)
        s_out_ref[0, h] = state


def _retention(lg, q, k, v, gr, gain, bias, layer, state0, state_offset, chunk, n_real, heads):
    b, s, _ = q.shape
    hd = KEY_DIM_R
    width = heads * hd
    seq_spec = pl.BlockSpec((1, s, width), lambda bi, g: (bi, 0, g))
    vec_spec = pl.BlockSpec((None, 1, width), lambda bi, g: (layer, 0, g))
    st_spec = pl.BlockSpec((1, heads, hd, hd), lambda bi, g: (bi, g, 0, 0))
    st_in_spec = pl.BlockSpec((1, heads, hd, hd), lambda bi, g: (state_offset + bi, g, 0, 0))
    return pl.pallas_call(
        functools.partial(_retention_body, chunk=chunk, n_real=n_real, n_chunks=s // chunk,
                          heads=heads),
        grid=(b, N_HEADS_R // heads),
        in_specs=[pl.BlockSpec((heads, 1, LANES), lambda bi, g: (g, 0, 0)),
                  seq_spec, seq_spec, seq_spec, seq_spec, vec_spec, vec_spec, st_in_spec],
        out_specs=[seq_spec, st_spec],
        out_shape=[jax.ShapeDtypeStruct((b, s, W_R), BF16),
                   jax.ShapeDtypeStruct((b, N_HEADS_R, hd, hd), F32)],
        compiler_params=_params("parallel", "parallel"),
        name="retention",
    )(lg, q, k, v, gr, gain, bias, state0)


def kernel(x_prompt, x_sample, cache_k, cache_v, state_ret, page_table, g_ffn1, w1_ffn1, w3_ffn1,
           w2_ffn1, g_mix, w_in, g_q, g_k, gn_gain, gn_bias, w_pa, w_pr, w_o, g_ffn2, w1_ffn2,
           w3_ffn2, w2_ffn2):
    b, s, _ = x_prompt.shape
    bd, sd, _ = x_sample.shape
    depth = w_in.shape[0]
    n_pool, page = cache_k.shape[1], cache_k.shape[2]
    mp, ms = b * s, bd * SAMPLE_ROWS
    tm_p, tm_s = 512, ms

    xp = x_prompt.reshape(mp, D_MODEL)
    xs = jnp.pad(x_sample, ((0, 0), (0, SAMPLE_ROWS - sd), (0, 0))).reshape(ms, D_MODEL)

    cache_kt = jnp.transpose(cache_k, (0, 1, 3, 4, 2)).reshape(depth, n_pool, W_A, page)
    cache_vt = jnp.transpose(cache_v, (0, 1, 3, 4, 2)).reshape(depth, n_pool, W_A, page)

    tables = _moba_tables(s)
    lg = _log_decay_rows()
    head_ones = jnp.kron(jnp.eye(N_HEADS_A, dtype=F32),
                         jnp.ones((HEAD_DIM_A, HEAD_DIM_A), F32)).astype(BF16)
    zero_state = jnp.zeros((b, N_HEADS_R, KEY_DIM_R, VAL_DIM_R), F32)
    states_in = state_ret.reshape(depth * bd, N_HEADS_R, KEY_DIM_R, VAL_DIM_R)

    vec = lambda a: a.reshape(depth, 1, -1)
    g1, gm, g2, gain, bias = vec(g_ffn1), vec(g_mix), vec(g_ffn2), vec(gn_gain), vec(gn_bias)
    gq, gk = vec(jnp.tile(g_q, (1, N_HEADS_A))), vec(jnp.tile(g_k, (1, N_HEADS_A)))
    w1a, w3a, w2a = w1_ffn1.astype(BF16), w3_ffn1.astype(BF16), w2_ffn1.astype(BF16)
    w1b, w3b, w2b = w1_ffn2.astype(BF16), w3_ffn2.astype(BF16), w2_ffn2.astype(BF16)
    win, wpa, wpr, wo = w_in.astype(BF16), w_pa.astype(BF16), w_pr.astype(BF16), w_o.astype(BF16)

    kv_p = kv_s = None
    sp_l, ss_l = [], []
    for l in range(depth):
        xp = _ffn(xp, l, g1, w1a, w3a, w2a, tm_p)
        q, kf, kb, vf, vb, qr, kr, vr, gr, ga, gb, km = _inproj(
            xp, l, depth, gm, win, gq, gk, head_ones, tm_p, kv_p, seq_len=s)
        kv_p = (kf, vf)
        r3 = lambda a: a.reshape(b, s, a.shape[-1])
        oa = _moba_prompt(r3(q), kb, r3(vb), km.reshape(b, s // MOBA_BLOCK, W_A), tables)
        o_r, st = _retention(lg, r3(qr), r3(kr), r3(vr), r3(gr), gain, bias, l, zero_state, 0,
                             RET_CHUNK_PROMPT, RET_CHUNK_PROMPT, RET_HEADS_PROMPT)
        xp = _outproj(oa.reshape(mp, W_A), o_r.reshape(mp, W_R), ga, gb, xp, l, wpa, wpr, wo, tm_p)
        sp_l.append(st)

        xs = _ffn_few_rows(xs, l, g1, w1a, w3a, w2a)
        q, kf, kb, vf, vb, qr, kr, vr, gr, ga, gb = _inproj(
            xs, l, depth, gm, win, gq, gk, head_ones, tm_s, kv_s)
        kv_s = (kf, vf)
        r3 = lambda a: a.reshape(bd, SAMPLE_ROWS, a.shape[-1])
        xp, oa = _ffn_moba_sample(xp, l, g2, w1b, w3b, w2b, tm_p, page_table, r3(q), r3(kb), r3(vb),
                                  cache_kt, cache_vt, sd)
        o_r, st = _retention(lg, r3(qr), r3(kr), r3(vr), r3(gr), gain, bias, l, states_in, l * bd,
                             SAMPLE_ROWS, sd, N_HEADS_R)
        xs = _outproj(oa.reshape(ms, W_A), o_r.reshape(ms, W_R), ga, gb, xs, l, wpa, wpr, wo, tm_s)
        xs = _ffn_few_rows(xs, l, g2, w1b, w3b, w2b)
        ss_l.append(st)

    y_prompt = xp.reshape(b, s, D_MODEL)
    y_sample = xs.reshape(bd, SAMPLE_ROWS, D_MODEL)[:, :sd]
    seq_minor = lambda a: a.reshape(depth, b, N_HEADS_A, HEAD_DIM_A, s).transpose(0, 1, 4, 2, 3)
    k_prompt, v_prompt = seq_minor(kv_p[0]), seq_minor(kv_p[1])
    heads5 = lambda a: a.reshape(depth, bd, SAMPLE_ROWS, N_HEADS_A, HEAD_DIM_A)[:, :, :sd]
    k_sample, v_sample = heads5(kv_s[0]), heads5(kv_s[1])
    return (y_prompt, y_sample, k_prompt, v_prompt, jnp.stack(sp_l),
            k_sample, v_sample, jnp.stack(ss_l))
```

```python
import functools

import jax
import jax.numpy as jnp
from jax import lax
from jax.experimental import pallas as pl
from jax.experimental.pallas import tpu as pltpu

F32 = jnp.float32
BF16 = jnp.bfloat16

D_MODEL = 1024
D_FF = 2816
N_HEADS_A = 8
HEAD_DIM_A = 64
MOBA_BLOCK = 256
MOBA_TOPK = 3
N_HEADS_R = 4
KEY_DIM_R = 128
VAL_DIM_R = 128
EPS = 1e-6
NEG_INF = -1e30
W_A = N_HEADS_A * HEAD_DIM_A
W_R = N_HEADS_R * KEY_DIM_R

LANES = 128
VMEM_LIMIT_BYTES = 56 * 1024 * 1024
SAMPLE_ROWS = 16
RET_CHUNK_PROMPT = 256
RET_HEADS_PROMPT = 2
PAGE_SLOTS = 32
PAGE_GROUP = 16
FFN_CHUNK = 256


def _nt(a, b):
    return lax.dot_general(a, b, (((1,), (1,)), ((), ())), preferred_element_type=F32)


def _tn(a, b):
    return lax.dot_general(a, b, (((0,), (0,)), ((), ())), preferred_element_type=F32)


def _mm(a, b):
    return jnp.dot(a, b, preferred_element_type=F32)


def _rms(x, g):
    return x * lax.rsqrt(jnp.mean(x * x, axis=-1, keepdims=True) + EPS) * g


def _split_bf16(x):
    hi = x.astype(BF16)
    lo = (x - hi.astype(F32)).astype(BF16)
    return hi, lo


def _params(*sem):
    return pltpu.CompilerParams(dimension_semantics=sem, vmem_limit_bytes=VMEM_LIMIT_BYTES)


def _row_spec(tm, cols):
    return pl.BlockSpec((tm, cols), lambda i: (i, 0))


def _const_spec(shape):
    return pl.BlockSpec(shape, lambda i: (0,) * len(shape), pipeline_mode=pl.Buffered(1))


def _layer_spec(layer, shape):
    return pl.BlockSpec((None,) + shape, lambda i: (layer, 0, 0), pipeline_mode=pl.Buffered(1))


def _ffn_body(x_ref, g_ref, w1_ref, w3_ref, w2_ref, o_ref):
    x = x_ref[...]
    h = _rms(x, g_ref[...]).astype(BF16)
    a = _mm(h, w1_ref[...])
    b = _mm(h, w3_ref[...])
    act = (a * jax.nn.sigmoid(a) * b).astype(BF16)
    o_ref[...] = x + 0.5 * _mm(act, w2_ref[...])


def _ffn(x, layer, g, w1, w3, w2, tm):
    m = x.shape[0]
    return pl.pallas_call(
        _ffn_body,
        grid=(m // tm,),
        in_specs=[_row_spec(tm, D_MODEL), _layer_spec(layer, (1, D_MODEL)),
                  _layer_spec(layer, (D_MODEL, D_FF)), _layer_spec(layer, (D_MODEL, D_FF)),
                  _layer_spec(layer, (D_FF, D_MODEL))],
        out_specs=_row_spec(tm, D_MODEL),
        out_shape=jax.ShapeDtypeStruct((m, D_MODEL), F32),
        compiler_params=_params("parallel"),
        name="ffn",
    )(x, g, w1, w3, w2)


def _ffn_few_rows_body(x_ref, g_ref, w1_ref, w3_ref, w2_ref, o_ref, h_ref, acc_ref):
    c = pl.program_id(0)

    @pl.when(c == 0)
    def _():
        h_ref[...] = _rms(x_ref[...], g_ref[...]).astype(BF16)
        acc_ref[...] = jnp.zeros_like(acc_ref)

    h = h_ref[...]
    a = _mm(h, w1_ref[...])
    act = (a * jax.nn.sigmoid(a) * _mm(h, w3_ref[...])).astype(BF16)
    acc_ref[...] += _mm(act, w2_ref[...])

    @pl.when(c == pl.num_programs(0) - 1)
    def _():
        o_ref[...] = x_ref[...] + 0.5 * acc_ref[...]


def _ffn_few_rows(x, layer, g, w1, w3, w2):
    m = x.shape[0]
    whole = pl.BlockSpec((m, D_MODEL), lambda c: (0, 0))
    return pl.pallas_call(
        _ffn_few_rows_body,
        grid=(D_FF // FFN_CHUNK,),
        in_specs=[whole, pl.BlockSpec((None, 1, D_MODEL), lambda c: (layer, 0, 0)),
                  pl.BlockSpec((None, D_MODEL, FFN_CHUNK), lambda c: (layer, 0, c)),
                  pl.BlockSpec((None, D_MODEL, FFN_CHUNK), lambda c: (layer, 0, c)),
                  pl.BlockSpec((None, FFN_CHUNK, D_MODEL), lambda c: (layer, c, 0))],
        out_specs=whole,
        out_shape=jax.ShapeDtypeStruct((m, D_MODEL), F32),
        scratch_shapes=[pltpu.VMEM((m, D_MODEL), BF16), pltpu.VMEM((m, D_MODEL), F32)],
        compiler_params=_params("arbitrary"),
        name="ffn_few_rows",
    )(x, g, w1, w3, w2)


_IN_COLS = (W_A, W_A, W_A, W_R, W_R, W_R, W_R, D_MODEL, D_MODEL)
_IN_OFFS = tuple(sum(_IN_COLS[:i]) for i in range(len(_IN_COLS) + 1))
D_IN = _IN_OFFS[-1]


def _inproj_body(x_ref, g_ref, w_ref, gq_ref, gk_ref, e_ref, *rest, seq_major):
    n_out = 12 if seq_major else 11
    q_o, kf_o, kb_o, vf_o, vb_o, qr_o, kr_o, vr_o, gr_o, ga_o, gb_o = rest[-n_out:][:11]
    h = _rms(x_ref[...], g_ref[...]).astype(BF16)

    def proj(i):
        return _mm(h, w_ref[:, _IN_OFFS[i]:_IN_OFFS[i + 1]])

    def head_rms(z, gain):
        ss = _mm((z * z).astype(BF16), e_ref[...])
        return z * lax.rsqrt(ss * (1.0 / HEAD_DIM_A) + EPS) * gain

    q_o[...] = head_rms(proj(0), gq_ref[...]) * (HEAD_DIM_A ** -0.5)
    k = head_rms(proj(1), gk_ref[...])
    v = proj(2)
    vb_o[...] = v.astype(BF16)
    if seq_major:
        kt = k.T
        kf_o[...] = kt
        kb_o[...] = kt.astype(BF16)
        vf_o[...] = v.T
        km_o = rest[-1]
        for r in range(km_o.shape[0]):
            km_o[r] = jnp.mean(k[r * MOBA_BLOCK:(r + 1) * MOBA_BLOCK, :], axis=0, keepdims=True)
    else:
        kf_o[...] = k
        kb_o[...] = k.astype(BF16)
        vf_o[...] = v
    qr_o[...] = proj(3).astype(BF16)
    kr_o[...] = (proj(4) * (KEY_DIM_R ** -0.5)).astype(BF16)
    vr_o[...] = proj(5).astype(BF16)
    gr_o[...] = proj(6)
    ga_o[...] = proj(7)
    gb_o[...] = proj(8)


def _inproj(x, layer, depth, g, w, gq, gk, e, tm, kv_stacks=None, seq_len=None):
    m = x.shape[0]
    outs = [(W_A, F32), (W_A, F32), (W_A, BF16), (W_A, F32), (W_A, BF16),
            (W_R, BF16), (W_R, BF16), (W_R, BF16), (W_R, F32), (D_MODEL, F32), (D_MODEL, F32)]
    stacked = (1, 3)
    out_specs = [_row_spec(tm, c) for c, _ in outs]
    out_shape = [jax.ShapeDtypeStruct((m, c), dt) for c, dt in outs]
    for n in stacked:
        if seq_len is None:
            out_specs[n] = pl.BlockSpec((None, tm, W_A), lambda i: (layer, i, 0))
            out_shape[n] = jax.ShapeDtypeStruct((depth, m, W_A), F32)
        else:
            tiles = seq_len // tm
            out_specs[n] = pl.BlockSpec((None, None, W_A, tm),
                                        lambda i: (layer, i // tiles, 0, i % tiles))
            out_shape[n] = jax.ShapeDtypeStruct((depth, m // seq_len, W_A, seq_len), F32)
    if seq_len is not None:
        out_specs[2] = pl.BlockSpec((None, W_A, tm), lambda i: (i // tiles, 0, i % tiles))
        out_shape[2] = jax.ShapeDtypeStruct((m // seq_len, W_A, seq_len), BF16)
        out_specs.append(pl.BlockSpec((tm // MOBA_BLOCK, 1, W_A), lambda i: (i, 0, 0)))
        out_shape.append(jax.ShapeDtypeStruct((m // MOBA_BLOCK, 1, W_A), F32))
    in_specs = [_row_spec(tm, D_MODEL), _layer_spec(layer, (1, D_MODEL)),
                _layer_spec(layer, (D_MODEL, D_IN)), _layer_spec(layer, (1, W_A)),
                _layer_spec(layer, (1, W_A)), _const_spec((W_A, W_A))]
    args = [x, g, w, gq, gk, e]
    aliases = {}
    if kv_stacks is not None:
        for n, stack in zip(stacked, kv_stacks):
            aliases[len(args)] = n
            in_specs.append(pl.BlockSpec(memory_space=pl.ANY))
            args.append(stack)
    return pl.pallas_call(
        functools.partial(_inproj_body, seq_major=seq_len is not None),
        grid=(m // tm,),
        in_specs=in_specs,
        out_specs=out_specs,
        out_shape=out_shape,
        input_output_aliases=aliases,
        compiler_params=_params("parallel"),
        name="inproj",
    )(*args)


def _outproj_body(oa_ref, or_ref, ga_ref, gb_ref, x_ref, wpa_ref, wpr_ref, wo_ref, o_ref):
    a = _mm(oa_ref[...], wpa_ref[...])
    r = _mm(or_ref[...], wpr_ref[...])
    merged = jax.nn.sigmoid(ga_ref[...]) * a + jax.nn.sigmoid(gb_ref[...]) * r
    o_ref[...] = x_ref[...] + _mm(merged.astype(BF16), wo_ref[...])


def _outproj(oa, o_r, ga, gb, x, layer, wpa, wpr, wo, tm):
    m = x.shape[0]
    return pl.pallas_call(
        _outproj_body,
        grid=(m // tm,),
        in_specs=[_row_spec(tm, W_A), _row_spec(tm, W_R), _row_spec(tm, D_MODEL),
                  _row_spec(tm, D_MODEL), _row_spec(tm, D_MODEL),
                  _layer_spec(layer, (W_A, D_MODEL)), _layer_spec(layer, (W_R, D_MODEL)),
                  _layer_spec(layer, (D_MODEL, D_MODEL))],
        out_specs=_row_spec(tm, D_MODEL),
        out_shape=jax.ShapeDtypeStruct((m, D_MODEL), F32),
        compiler_params=_params("parallel"),
        name="outproj",
    )(oa, o_r, ga, gb, x, wpa, wpr, wo)


def _rank_select(gate, lane, n_valid_mask, candidates):
    gate = jnp.where(n_valid_mask, gate, NEG_INF)
    cnt = jnp.zeros(gate.shape, F32)
    for c in candidates:
        gc = gate[:, c:c + 1]
        beats = (gc > gate) | ((gc == gate) & (lane > c))
        cnt = cnt + jnp.where(beats, 1.0, 0.0)
    return jnp.where((cnt < MOBA_TOPK) & n_valid_mask, 1.0, 0.0)


def _moba_tables(s):
    nb = s // MOBA_BLOCK
    pos = jnp.arange(s)
    blk = (pos // MOBA_BLOCK).astype(F32)
    within = (pos % MOBA_BLOCK).astype(F32)
    ind = (pos[:, None] // MOBA_BLOCK == jnp.arange(nb)[None, :]).astype(F32)
    slopes = 2.0 ** (-8.0 * (jnp.arange(N_HEADS_A, dtype=F32) + 1.0) / N_HEADS_A)

    def half(slope):
        cols = [ind, (slope * MOBA_BLOCK * blk)[:, None], (slope * within)[:, None],
                jnp.zeros((s, HEAD_DIM_A - nb - 2), F32)]
        return jnp.concatenate(cols, axis=1)

    pairs = [jnp.concatenate([half(slopes[2 * p + 1]), half(slopes[2 * p])], axis=1).T
             for p in range(N_HEADS_A // 2)]
    return jnp.stack(pairs).astype(BF16)


def _moba_prompt_body(q_ref, k_ref, v_ref, km_ref, tab_ref, o_ref, qa_ref, qb_ref, ka_ref, kb_ref,
                      va_ref, vb_ref, *, nb):
    blk = MOBA_BLOCK
    s_len = nb * blk
    q2 = q_ref[0]
    lane = lax.broadcasted_iota(jnp.int32, (1, LANES), 1)
    lo_half = lane < HEAD_DIM_A

    km = km_ref[0]
    km_parts = []
    for part in (jnp.where(lo_half, km, 0.0), jnp.where(lo_half, 0.0, km)):
        km_parts.extend(_split_bf16(part))
    kms = jnp.concatenate(km_parts, axis=0)
    q_hi, q_lo = _split_bf16(q2)
    r_hi = _nt(kms, q_hi)
    r_lo = _nt(kms, q_lo)
    gates = [r_hi[0:nb] + r_hi[nb:2 * nb] + r_lo[0:nb],
             r_hi[2 * nb:3 * nb] + r_hi[3 * nb:4 * nb] + r_lo[2 * nb:3 * nb]]

    brow = lax.broadcasted_iota(jnp.int32, (nb, s_len), 0)
    own = lax.broadcasted_iota(jnp.int32, (nb, s_len), 1) // blk
    past = brow < own

    def block_bias(gate):
        g = jnp.where(past, gate, NEG_INF)
        cnt = jnp.zeros((nb, s_len), F32)
        for c in range(nb):
            gc = g[c:c + 1, :]
            beats = (gc > g) | ((gc == g) & (brow > c))
            cnt = cnt + jnp.where(beats, 1.0, 0.0)
        allowed = ((cnt < MOBA_TOPK) & past) | (brow == own)
        return jnp.where(allowed, 0.0, NEG_INF)

    ones2 = jnp.where(lax.broadcasted_iota(jnp.int32, (8, s_len), 0) < 2, 1.0, 0.0)
    pad = jnp.zeros((HEAD_DIM_A - nb - 8, s_len), F32)
    ext_t = jnp.concatenate([block_bias(gates[1]), ones2, pad, block_bias(gates[0]), ones2, pad], axis=0)
    ext = ext_t.T
    qa_ref[...] = jnp.where(lo_half, q2, ext).astype(BF16)
    qb_ref[...] = jnp.where(lo_half, ext, q2).astype(BF16)
    k2 = k_ref[0]
    tab = tab_ref[0]
    lo_rows = lax.broadcasted_iota(jnp.int32, (LANES, 1), 0) < HEAD_DIM_A
    ka_ref[...] = jnp.where(lo_rows, k2, tab)
    kb_ref[...] = jnp.where(lo_rows, tab, k2)
    v2 = v_ref[0]
    sum_lane_a, sum_lane_b = HEAD_DIM_A, 0
    va_ref[...] = jnp.where(lo_half, v2, jnp.where(lane == sum_lane_a, 1.0, 0.0).astype(BF16))
    vb_ref[...] = jnp.where(lo_half, jnp.where(lane == sum_lane_b, 1.0, 0.0).astype(BF16), v2)

    row = lax.broadcasted_iota(jnp.int32, (blk, blk), 0)
    col = lax.broadcasted_iota(jnp.int32, (blk, blk), 1)
    causal = col <= row

    heads = ((qa_ref, ka_ref, va_ref, sum_lane_a), (qb_ref, kb_ref, vb_ref, sum_lane_b))
    items = [(j, h) for j in range(nb) for h in range(2)]

    def scores(j, h):
        qx_ref, kx_ref, _, _ = heads[h]
        lo, hi = j * blk, (j + 1) * blk
        qx = qx_ref[lo:hi, :]
        s_own = jnp.where(causal, _mm(qx, kx_ref[:, lo:hi]), NEG_INF)
        s_past = _mm(qx, kx_ref[:, 0:lo]) if j > 0 else None
        return s_own, s_past

    def probs(j, s_own, s_past):
        m = jnp.max(s_own, axis=-1, keepdims=True)
        if j > 0:
            m = jnp.maximum(m, jnp.max(s_past, axis=-1, keepdims=True))
        p_own = jnp.exp(s_own - m).astype(BF16)
        p_past = jnp.exp(s_past - m).astype(BF16) if j > 0 else None
        return p_own, p_past

    def weighted(j, h, p_own, p_past):
        _, _, vx_ref, sum_lane = heads[h]
        lo, hi = j * blk, (j + 1) * blk
        acc = _mm(p_own, vx_ref[lo:hi, :])
        if j > 0:
            acc = acc + _mm(p_past, vx_ref[0:lo, :])
        return acc / acc[:, sum_lane:sum_lane + 1]

    outs = []

    def finish(j, h, p):
        outs.append(weighted(j, h, *p))
        if h == 1:
            lo, hi = j * blk, (j + 1) * blk
            o_ref[0, lo:hi, :] = jnp.where(lo_half, outs[0], outs[1]).astype(o_ref.dtype)
            outs.clear()

    ahead = scores(*items[0])
    behind = None
    for k, (j, h) in enumerate(items):
        current = ahead
        if k + 1 < len(items):
            ahead = scores(*items[k + 1])
        p = probs(j, *current)
        if behind:
            finish(*behind)
        behind = (j, h, p)
    finish(*behind)


def _moba_prompt(q, kb, vb, km, tables):
    b, s, _ = q.shape
    nb = s // MOBA_BLOCK
    assert nb == 8, "block-indicator columns are laid out for 8 MoBA blocks"
    npair = N_HEADS_A // 2
    seq_spec = pl.BlockSpec((1, s, LANES), lambda bi, p: (bi, 0, p))
    return pl.pallas_call(
        functools.partial(_moba_prompt_body, nb=nb),
        grid=(b, npair),
        in_specs=[seq_spec, pl.BlockSpec((1, LANES, s), lambda bi, p: (bi, p, 0)), seq_spec,
                  pl.BlockSpec((1, nb, LANES), lambda bi, p: (bi, 0, p)),
                  pl.BlockSpec((1, LANES, s), lambda bi, p: (p, 0, 0))],
        out_specs=seq_spec,
        out_shape=jax.ShapeDtypeStruct((b, s, W_A), BF16),
        scratch_shapes=[pltpu.VMEM((s, LANES), BF16)] * 2 + [pltpu.VMEM((LANES, s), BF16)] * 2
        + [pltpu.VMEM((s, LANES), BF16)] * 2,
        compiler_params=_params("parallel", "parallel"),
        name="moba_prompt",
    )(q, kb, vb, km, tables)


def _moba_sample_stages(pt_ref, q_ref, kn_ref, vn_ref, kt_hbm, vt_hbm, o_ref, buf, sem, s_ref,
                        p_ref, *, layer, n_real, past_len):
    b = pl.program_id(0)
    n_seq = pl.num_programs(0)
    n_pages = s_ref.shape[0]
    ring = PAGE_SLOTS
    rows = N_HEADS_A * n_real

    def page_copy(hbm, seq, pg):
        slot = pg % ring
        return pltpu.make_async_copy(hbm.at[layer, pt_ref[seq, pg]], buf.at[slot], sem.at[slot])

    @pl.when(b == 0)
    def _():
        for pg in range(ring):
            page_copy(kt_hbm, b, pg).start()

    rowi = lax.broadcasted_iota(jnp.int32, (rows, W_A), 0)
    lanei = lax.broadcasted_iota(jnp.int32, (rows, W_A), 1)
    head_mask = (lanei >> 6) == (rowi & (N_HEADS_A - 1))
    q = q_ref[0]
    q_rep = jnp.concatenate(
        [jnp.broadcast_to(q[t:t + 1, :], (N_HEADS_A, W_A)) for t in range(n_real)], axis=0)
    q_hi, q_lo = _split_bf16(jnp.where(head_mask, q_rep, 0.0))
    q2 = jnp.concatenate([q_hi, q_lo], axis=0)

    def raw_scores(keys_t=None, keys=None):
        s2 = _mm(q2, keys_t) if keys_t is not None else _nt(q2, keys)
        return s2[:rows] + s2[rows:]

    lane = lax.broadcasted_iota(jnp.int32, (1, LANES), 1)

    def start_group(hbm, seq, first):
        for pg in range(first, first + PAGE_GROUP):
            page_copy(hbm, seq, pg).start()

    def wait_group(hbm, first):
        for pg in range(first, first + PAGE_GROUP):
            page_copy(hbm, b, pg).wait()

    st = {"gsum": jnp.zeros((rows, LANES), F32)}

    def k_group(first):
        gsum = st["gsum"]
        for pg in range(first, first + PAGE_GROUP):
            s = raw_scores(keys_t=buf[pg % ring].astype(BF16))
            s_ref[pg] = s
            gsum = jnp.where(lane == pg, jnp.sum(s, axis=-1, keepdims=True), gsum)
        st["gsum"] = gsum

    def k_refill(first):
        if first + ring < n_pages:
            start_group(kt_hbm, b, first + ring)
        else:
            start_group(vt_hbm, b, first + ring - n_pages)

    def select_and_softmax():
        gsum = st["gsum"]
        gate = (gsum + pltpu.roll(gsum, LANES - 1, 1)) * (1.0 / MOBA_BLOCK)
        valid = ((lane & 1) == 0) & (lane < n_pages)
        sel = _rank_select(gate, lane, valid, range(0, n_pages, 2))

        rcol = lax.broadcasted_iota(jnp.int32, (rows, 1), 0)
        slope = jnp.exp2(-((rcol & (N_HEADS_A - 1)) + 1).astype(F32))
        qpos = past_len + (rcol >> 3)

        m_lanes = jnp.full((rows, LANES), NEG_INF, F32)
        for blk in range(n_pages // 2):
            chosen = sel[:, 2 * blk:2 * blk + 1] > 0.5
            for pg in (2 * blk, 2 * blk + 1):
                kpos = pg * LANES + lane
                s = s_ref[pg] - slope * (qpos - kpos).astype(F32)
                s = jnp.where(chosen, s, NEG_INF)
                s_ref[pg] = s
                m_lanes = jnp.maximum(m_lanes, s)
        tnew = lax.broadcasted_iota(jnp.int32, (1, SAMPLE_ROWS), 1)
        dist = (rcol >> 3) - tnew
        s_own = raw_scores(keys=kn_ref[0]) - slope * dist.astype(F32)
        s_own = jnp.where(dist >= 0, s_own, NEG_INF)
        m = jnp.maximum(jnp.max(m_lanes, axis=-1, keepdims=True),
                        jnp.max(s_own, axis=-1, keepdims=True))

        l_lanes = jnp.zeros((rows, LANES), F32)
        for pg in range(n_pages):
            p = jnp.exp(s_ref[pg] - m)
            l_lanes = l_lanes + p
            p_ref[pg] = p.astype(BF16)
        p_own = jnp.exp(s_own - m)
        st["l"] = jnp.sum(l_lanes, axis=-1, keepdims=True) + jnp.sum(p_own, axis=-1, keepdims=True)
        st["acc"] = _mm(p_own.astype(BF16), vn_ref[0])

    def v_group(first):
        acc = st["acc"]
        for pg in range(first, first + PAGE_GROUP):
            acc = acc + _nt(p_ref[pg], buf[pg % ring].astype(BF16))
        st["acc"] = acc

    def v_refill(first):
        if first + ring < n_pages:
            start_group(vt_hbm, b, first + ring)
        else:
            @pl.when(b + 1 < n_seq)
            def _():
                start_group(kt_hbm, b + 1, first + ring - n_pages)

    def finish():
        o = jnp.where(head_mask, st["acc"] / st["l"], 0.0)
        out_rows = [jnp.sum(o[N_HEADS_A * t:N_HEADS_A * (t + 1), :], axis=0, keepdims=True)
                    for t in range(n_real)]
        out_rows.append(jnp.zeros((SAMPLE_ROWS - n_real, W_A), F32))
        o_ref[0] = jnp.concatenate(out_rows, axis=0).astype(o_ref.dtype)

    part = functools.partial
    nothing = lambda: None
    groups = range(0, n_pages, PAGE_GROUP)
    return ([(part(wait_group, kt_hbm, f), part(k_group, f), part(k_refill, f)) for f in groups]
            + [(nothing, select_and_softmax, nothing)]
            + [(part(wait_group, vt_hbm, f), part(v_group, f), part(v_refill, f)) for f in groups]
            + [(nothing, finish, nothing)])


def _ffn_moba_sample_body(pt_ref, x_ref, g_ref, w1_ref, w3_ref, w2_ref, q_ref, kn_ref, vn_ref,
                          kt_hbm, vt_hbm, y_ref, o_ref, buf, sem, s_ref, p_ref,
                          *, layer, n_real, past_len):
    stages = _moba_sample_stages(pt_ref, q_ref, kn_ref, vn_ref, kt_hbm, vt_hbm, o_ref, buf, sem,
                                 s_ref, p_ref, layer=layer, n_real=n_real, past_len=past_len)
    x = x_ref[...]
    h = _rms(x, g_ref[...]).astype(BF16)
    n_chunks = D_FF // FFN_CHUNK
    assert len(stages) <= n_chunks
    acc = jnp.zeros(x.shape, F32)
    for c in range(n_chunks):
        cols = slice(c * FFN_CHUNK, (c + 1) * FFN_CHUNK)
        a = _mm(h, w1_ref[:, cols])
        gate = _mm(h, w3_ref[:, cols])
        act = (a * jax.nn.sigmoid(a) * gate).astype(BF16)
        acc = acc + _mm(act, w2_ref[cols, :])
        if c < len(stages):
            for part in stages[c]:
                part()
    y_ref[...] = x + 0.5 * acc


def _ffn_moba_sample(x, layer, g, w1, w3, w2, tm, page_table, q, kn, vn, cache_kt, cache_vt, n_real):
    m = x.shape[0]
    bd, n_pages = page_table.shape
    page = cache_kt.shape[-1]
    assert m // tm == bd, "one ffn row tile per sample sequence"
    assert page == LANES and MOBA_BLOCK == 2 * page
    assert PAGE_SLOTS <= n_pages <= LANES and n_pages % PAGE_GROUP == 0 == PAGE_SLOTS % PAGE_GROUP
    rows = N_HEADS_A * n_real

    def layer_spec(shape):
        return pl.BlockSpec((None,) + shape, lambda i, pt: (layer, 0, 0), pipeline_mode=pl.Buffered(1))

    row_spec = pl.BlockSpec((tm, D_MODEL), lambda i, pt: (i, 0))
    tok_spec = pl.BlockSpec((1, SAMPLE_ROWS, W_A), lambda i, pt: (i, 0, 0))
    hbm_spec = pl.BlockSpec(memory_space=pl.ANY)
    grid_spec = pltpu.PrefetchScalarGridSpec(
        num_scalar_prefetch=1,
        grid=(bd,),
        in_specs=[row_spec, layer_spec((1, D_MODEL)), layer_spec((D_MODEL, D_FF)),
                  layer_spec((D_MODEL, D_FF)), layer_spec((D_FF, D_MODEL)),
                  tok_spec, tok_spec, tok_spec, hbm_spec, hbm_spec],
        out_specs=[row_spec, tok_spec],
        scratch_shapes=[pltpu.VMEM((PAGE_SLOTS, W_A, page), F32),
                        pltpu.SemaphoreType.DMA((PAGE_SLOTS,)),
                        pltpu.VMEM((n_pages, rows, LANES), F32),
                        pltpu.VMEM((n_pages, rows, LANES), BF16)],
    )
    return pl.pallas_call(
        functools.partial(_ffn_moba_sample_body, layer=layer, n_real=n_real,
                          past_len=n_pages * page),
        grid_spec=grid_spec,
        out_shape=[jax.ShapeDtypeStruct((m, D_MODEL), F32),
                   jax.ShapeDtypeStruct((bd, SAMPLE_ROWS, W_A), BF16)],
        compiler_params=_params("arbitrary"),
        name="ffn_moba_sample",
    )(page_table, x, g, w1, w3, w2, q, kn, vn, cache_kt, cache_vt)


def _log_decay_rows():
    log_decay = jnp.log(1.0 - 2.0 ** (-5.0 - jnp.arange(N_HEADS_R, dtype=F32)))
    return jnp.broadcast_to(log_decay.reshape(N_HEADS_R, 1, 1), (N_HEADS_R, 1, LANES))


def _retention_body(lg_ref, q_ref, k_ref, v_ref, gr_ref, gain_ref, bias_ref, s0_ref,
                    o_ref, s_out_ref, *, chunk, n_real, n_chunks, heads):
    hd = KEY_DIM_R
    ii = lax.broadcasted_iota(jnp.int32, (chunk, chunk), 0)
    jj = lax.broadcasted_iota(jnp.int32, (chunk, chunk), 1)
    diff = (ii - jj).astype(F32)
    icol = lax.broadcasted_iota(jnp.int32, (chunk, 1), 0).astype(F32)
    for h in range(heads):
        cols = slice(h * hd, (h + 1) * hd)
        lg = lg_ref[h][:, 0:1]
        dmat = jnp.where(diff >= 0, jnp.exp(lg * jnp.maximum(diff, 0.0)), 0.0)
        qdec = jnp.exp(lg * (icol + 1.0))
        kdec = jnp.where(icol < n_real, jnp.exp(lg * (n_real - 1.0 - icol)), 0.0)
        chunk_decay = jnp.exp(lg * float(n_real))
        gain = gain_ref[:, cols]
        bias = bias_ref[:, cols]
        state = s0_ref[0, h]
        for c in range(n_chunks):
            rows = slice(c * chunk, (c + 1) * chunk)
            qc = q_ref[0, rows, cols]
            kc = k_ref[0, rows, cols]
            vc = v_ref[0, rows, cols]
            inner = _nt(qc, kc) * dmat
            o = _mm(inner.astype(BF16), vc) + _mm(qc, state.astype(BF16)) * qdec
            kd = (kc.astype(F32) * kdec).astype(BF16)
            state = chunk_decay * state + _tn(kd, vc)
            mu = jnp.mean(o, axis=-1, keepdims=True)
            d = o - mu
            var = jnp.mean(d * d, axis=-1, keepdims=True)
            y = d * lax.rsqrt(var + EPS) * gain + bias
            gr = gr_ref[0, rows, cols]
            o_ref[0, rows, cols] = (y * (gr * jax.nn.sigmoid(gr))).astype(o_ref.dtype)
        s_out_ref[0, h] = state


def _retention(lg, q, k, v, gr, gain, bias, layer, state0, state_offset, chunk, n_real, heads):
    b, s, _ = q.shape
    hd = KEY_DIM_R
    width = heads * hd
    seq_spec = pl.BlockSpec((1, s, width), lambda bi, g: (bi, 0, g))
    vec_spec = pl.BlockSpec((None, 1, width), lambda bi, g: (layer, 0, g))
    st_spec = pl.BlockSpec((1, heads, hd, hd), lambda bi, g: (bi, g, 0, 0))
    st_in_spec = pl.BlockSpec((1, heads, hd, hd), lambda bi, g: (state_offset + bi, g, 0, 0))
    return pl.pallas_call(
        functools.partial(_retention_body, chunk=chunk, n_real=n_real, n_chunks=s // chunk,
                          heads=heads),
        grid=(b, N_HEADS_R // heads),
        in_specs=[pl.BlockSpec((heads, 1, LANES), lambda bi, g: (g, 0, 0)),
                  seq_spec, seq_spec, seq_spec, seq_spec, vec_spec, vec_spec, st_in_spec],
        out_specs=[seq_spec, st_spec],
        out_shape=[jax.ShapeDtypeStruct((b, s, W_R), BF16),
                   jax.ShapeDtypeStruct((b, N_HEADS_R, hd, hd), F32)],
        compiler_params=_params("parallel", "parallel"),
        name="retention",
    )(lg, q, k, v, gr, gain, bias, state0)


def kernel(x_prompt, x_sample, cache_k, cache_v, state_ret, page_table, g_ffn1, w1_ffn1, w3_ffn1,
           w2_ffn1, g_mix, w_in, g_q, g_k, gn_gain, gn_bias, w_pa, w_pr, w_o, g_ffn2, w1_ffn2,
           w3_ffn2, w2_ffn2):
    b, s, _ = x_prompt.shape
    bd, sd, _ = x_sample.shape
    depth = w_in.shape[0]
    n_pool, page = cache_k.shape[1], cache_k.shape[2]
    mp, ms = b * s, bd * SAMPLE_ROWS
    tm_p, tm_s = 512, ms

    xp = x_prompt.reshape(mp, D_MODEL)
    xs = jnp.pad(x_sample, ((0, 0), (0, SAMPLE_ROWS - sd), (0, 0))).reshape(ms, D_MODEL)

    cache_kt = jnp.transpose(cache_k, (0, 1, 3, 4, 2)).reshape(depth, n_pool, W_A, page)
    cache_vt = jnp.transpose(cache_v, (0, 1, 3, 4, 2)).reshape(depth, n_pool, W_A, page)

    tables = _moba_tables(s)
    lg = _log_decay_rows()
    head_ones = jnp.kron(jnp.eye(N_HEADS_A, dtype=F32),
                         jnp.ones((HEAD_DIM_A, HEAD_DIM_A), F32)).astype(BF16)
    zero_state = jnp.zeros((b, N_HEADS_R, KEY_DIM_R, VAL_DIM_R), F32)
    states_in = state_ret.reshape(depth * bd, N_HEADS_R, KEY_DIM_R, VAL_DIM_R)

    vec = lambda a: a.reshape(depth, 1, -1)
    g1, gm, g2, gain, bias = vec(g_ffn1), vec(g_mix), vec(g_ffn2), vec(gn_gain), vec(gn_bias)
    gq, gk = vec(jnp.tile(g_q, (1, N_HEADS_A))), vec(jnp.tile(g_k, (1, N_HEADS_A)))
    w1a, w3a, w2a = w1_ffn1.astype(BF16), w3_ffn1.astype(BF16), w2_ffn1.astype(BF16)
    w1b, w3b, w2b = w1_ffn2.astype(BF16), w3_ffn2.astype(BF16), w2_ffn2.astype(BF16)
    win, wpa, wpr, wo = w_in.astype(BF16), w_pa.astype(BF16), w_pr.astype(BF16), w_o.astype(BF16)

    kv_p = kv_s = None
    sp_l, ss_l = [], []
    for l in range(depth):
        xp = _ffn(xp, l, g1, w1a, w3a, w2a, tm_p)
        q, kf, kb, vf, vb, qr, kr, vr, gr, ga, gb, km = _inproj(
            xp, l, depth, gm, win, gq, gk, head_ones, tm_p, kv_p, seq_len=s)
        kv_p = (kf, vf)
        r3 = lambda a: a.reshape(b, s, a.shape[-1])
        oa = _moba_prompt(r3(q), kb, r3(vb), km.reshape(b, s // MOBA_BLOCK, W_A), tables)
        o_r, st = _retention(lg, r3(qr), r3(kr), r3(vr), r3(gr), gain, bias, l, zero_state, 0,
                             RET_CHUNK_PROMPT, RET_CHUNK_PROMPT, RET_HEADS_PROMPT)
        xp = _outproj(oa.reshape(mp, W_A), o_r.reshape(mp, W_R), ga, gb, xp, l, wpa, wpr, wo, tm_p)
        sp_l.append(st)

        xs = _ffn_few_rows(xs, l, g1, w1a, w3a, w2a)
        q, kf, kb, vf, vb, qr, kr, vr, gr, ga, gb = _inproj(
            xs, l, depth, gm, win, gq, gk, head_ones, tm_s, kv_s)
        kv_s = (kf, vf)
        r3 = lambda a: a.reshape(bd, SAMPLE_ROWS, a.shape[-1])
        xp, oa = _ffn_moba_sample(xp, l, g2, w1b, w3b, w2b, tm_p, page_table, r3(q), r3(kb), r3(vb),
                                  cache_kt, cache_vt, sd)
        o_r, st = _retention(lg, r3(qr), r3(kr), r3(vr), r3(gr), gain, bias, l, states_in, l * bd,
                             SAMPLE_ROWS, sd, N_HEADS_R)
        xs = _outproj(oa.reshape(ms, W_A), o_r.reshape(ms, W_R), ga, gb, xs, l, wpa, wpr, wo, tm_s)
        xs = _ffn_few_rows(xs, l, g2, w1b, w3b, w2b)
        ss_l.append(st)

    y_prompt = xp.reshape(b, s, D_MODEL)
    y_sample = xs.reshape(bd, SAMPLE_ROWS, D_MODEL)[:, :sd]
    seq_minor = lambda a: a.reshape(depth, b, N_HEADS_A, HEAD_DIM_A, s).transpose(0, 1, 4, 2, 3)
    k_prompt, v_prompt = seq_minor(kv_p[0]), seq_minor(kv_p[1])
    heads5 = lambda a: a.reshape(depth, bd, SAMPLE_ROWS, N_HEADS_A, HEAD_DIM_A)[:, :, :sd]
    k_sample, v_sample = heads5(kv_s[0]), heads5(kv_s[1])
    return (y_prompt, y_sample, k_prompt, v_prompt, jnp.stack(sp_l),
            k_sample, v_sample, jnp.stack(ss_l))
```

```python
import functools

import jax
import jax.numpy as jnp
from jax import lax
from jax.experimental import pallas as pl
from jax.experimental.pallas import tpu as pltpu

F32 = jnp.float32
BF16 = jnp.bfloat16

D_MODEL = 1024
D_FF = 2816
N_HEADS_A = 8
HEAD_DIM_A = 64
MOBA_BLOCK = 256
MOBA_TOPK = 3
N_HEADS_R = 4
KEY_DIM_R = 128
VAL_DIM_R = 128
EPS = 1e-6
NEG_INF = -1e30
W_A = N_HEADS_A * HEAD_DIM_A
W_R = N_HEADS_R * KEY_DIM_R

LANES = 128
VMEM_LIMIT_BYTES = 56 * 1024 * 1024
SAMPLE_ROWS = 16
RET_CHUNK_PROMPT = 256
RET_HEADS_PROMPT = 2
PAGE_SLOTS = 32
PAGE_GROUP = 16
FFN_CHUNK = 256


def _nt(a, b):
    return lax.dot_general(a, b, (((1,), (1,)), ((), ())), preferred_element_type=F32)


def _tn(a, b):
    return lax.dot_general(a, b, (((0,), (0,)), ((), ())), preferred_element_type=F32)


def _mm(a, b):
    return jnp.dot(a, b, preferred_element_type=F32)


def _rms(x, g):
    return x * lax.rsqrt(jnp.mean(x * x, axis=-1, keepdims=True) + EPS) * g


def _split_bf16(x):
    hi = x.astype(BF16)
    lo = (x - hi.astype(F32)).astype(BF16)
    return hi, lo


def _params(*sem):
    return pltpu.CompilerParams(dimension_semantics=sem, vmem_limit_bytes=VMEM_LIMIT_BYTES)


def _row_spec(tm, cols):
    return pl.BlockSpec((tm, cols), lambda i: (i, 0))


def _const_spec(shape):
    return pl.BlockSpec(shape, lambda i: (0,) * len(shape), pipeline_mode=pl.Buffered(1))


def _layer_spec(layer, shape):
    return pl.BlockSpec((None,) + shape, lambda i: (layer, 0, 0), pipeline_mode=pl.Buffered(1))


def _ffn_body(x_ref, g_ref, w1_ref, w3_ref, w2_ref, o_ref):
    x = x_ref[...]
    h = _rms(x, g_ref[...]).astype(BF16)
    a = _mm(h, w1_ref[...])
    b = _mm(h, w3_ref[...])
    act = (a * jax.nn.sigmoid(a) * b).astype(BF16)
    o_ref[...] = x + 0.5 * _mm(act, w2_ref[...])


def _ffn(x, layer, g, w1, w3, w2, tm):
    m = x.shape[0]
    return pl.pallas_call(
        _ffn_body,
        grid=(m // tm,),
        in_specs=[_row_spec(tm, D_MODEL), _layer_spec(layer, (1, D_MODEL)),
                  _layer_spec(layer, (D_MODEL, D_FF)), _layer_spec(layer, (D_MODEL, D_FF)),
                  _layer_spec(layer, (D_FF, D_MODEL))],
        out_specs=_row_spec(tm, D_MODEL),
        out_shape=jax.ShapeDtypeStruct((m, D_MODEL), F32),
        compiler_params=_params("parallel"),
        name="ffn",
    )(x, g, w1, w3, w2)


def _ffn_few_rows_body(x_ref, g_ref, w1_ref, w3_ref, w2_ref, o_ref, h_ref, acc_ref):
    c = pl.program_id(0)

    @pl.when(c == 0)
    def _():
        h_ref[...] = _rms(x_ref[...], g_ref[...]).astype(BF16)
        acc_ref[...] = jnp.zeros_like(acc_ref)

    h = h_ref[...]
    a = _mm(h, w1_ref[...])
    act = (a * jax.nn.sigmoid(a) * _mm(h, w3_ref[...])).astype(BF16)
    acc_ref[...] += _mm(act, w2_ref[...])

    @pl.when(c == pl.num_programs(0) - 1)
    def _():
        o_ref[...] = x_ref[...] + 0.5 * acc_ref[...]


def _ffn_few_rows(x, layer, g, w1, w3, w2):
    m = x.shape[0]
    whole = pl.BlockSpec((m, D_MODEL), lambda c: (0, 0))
    return pl.pallas_call(
        _ffn_few_rows_body,
        grid=(D_FF // FFN_CHUNK,),
        in_specs=[whole, pl.BlockSpec((None, 1, D_MODEL), lambda c: (layer, 0, 0)),
                  pl.BlockSpec((None, D_MODEL, FFN_CHUNK), lambda c: (layer, 0, c)),
                  pl.BlockSpec((None, D_MODEL, FFN_CHUNK), lambda c: (layer, 0, c)),
                  pl.BlockSpec((None, FFN_CHUNK, D_MODEL), lambda c: (layer, c, 0))],
        out_specs=whole,
        out_shape=jax.ShapeDtypeStruct((m, D_MODEL), F32),
        scratch_shapes=[pltpu.VMEM((m, D_MODEL), BF16), pltpu.VMEM((m, D_MODEL), F32)],
        compiler_params=_params("arbitrary"),
        name="ffn_few_rows",
    )(x, g, w1, w3, w2)


_IN_COLS = (W_A, W_A, W_A, W_R, W_R, W_R, W_R, D_MODEL, D_MODEL)
_IN_OFFS = tuple(sum(_IN_COLS[:i]) for i in range(len(_IN_COLS) + 1))
D_IN = _IN_OFFS[-1]


def _inproj_body(x_ref, g_ref, w_ref, gq_ref, gk_ref, e_ref, *rest, seq_major):
    n_out = 12 if seq_major else 11
    q_o, kf_o, kb_o, vf_o, vb_o, qr_o, kr_o, vr_o, gr_o, ga_o, gb_o = rest[-n_out:][:11]
    h = _rms(x_ref[...], g_ref[...]).astype(BF16)

    def proj(i):
        return _mm(h, w_ref[:, _IN_OFFS[i]:_IN_OFFS[i + 1]])

    def head_rms(z, gain):
        zz = (z * z).astype(BF16)
        half = W_A // 2
        ss = jnp.concatenate([_mm(zz[:, :half], e_ref[...]), _mm(zz[:, half:], e_ref[...])], axis=1)
        return z * lax.rsqrt(ss * (1.0 / HEAD_DIM_A) + EPS) * gain

    q_o[...] = head_rms(proj(0), gq_ref[...]) * (HEAD_DIM_A ** -0.5)
    k = head_rms(proj(1), gk_ref[...])
    v = proj(2)
    vb_o[...] = v.astype(BF16)
    if seq_major:
        kt = k.T
        kf_o[...] = kt
        kb_o[...] = kt.astype(BF16)
        vf_o[...] = v.T
        km_o = rest[-1]
        for r in range(km_o.shape[0]):
            km_o[r] = jnp.mean(k[r * MOBA_BLOCK:(r + 1) * MOBA_BLOCK, :], axis=0, keepdims=True)
    else:
        kf_o[...] = k
        kb_o[...] = k.astype(BF16)
        vf_o[...] = v
    qr_o[...] = proj(3).astype(BF16)
    kr_o[...] = (proj(4) * (KEY_DIM_R ** -0.5)).astype(BF16)
    vr_o[...] = proj(5).astype(BF16)
    gr_o[...] = proj(6).astype(BF16)
    ga_o[...] = proj(7).astype(BF16)
    gb_o[...] = proj(8).astype(BF16)


def _inproj(x, layer, depth, g, w, gq, gk, e, tm, kv_stacks=None, seq_len=None):
    m = x.shape[0]
    outs = [(W_A, F32), (W_A, F32), (W_A, BF16), (W_A, F32), (W_A, BF16),
            (W_R, BF16), (W_R, BF16), (W_R, BF16), (W_R, BF16), (D_MODEL, BF16), (D_MODEL, BF16)]
    stacked = (1, 3)
    out_specs = [_row_spec(tm, c) for c, _ in outs]
    out_shape = [jax.ShapeDtypeStruct((m, c), dt) for c, dt in outs]
    for n in stacked:
        if seq_len is None:
            out_specs[n] = pl.BlockSpec((None, tm, W_A), lambda i: (layer, i, 0))
            out_shape[n] = jax.ShapeDtypeStruct((depth, m, W_A), F32)
        else:
            tiles = seq_len // tm
            out_specs[n] = pl.BlockSpec((None, None, W_A, tm),
                                        lambda i: (layer, i // tiles, 0, i % tiles))
            out_shape[n] = jax.ShapeDtypeStruct((depth, m // seq_len, W_A, seq_len), F32)
    if seq_len is not None:
        out_specs[2] = pl.BlockSpec((None, W_A, tm), lambda i: (i // tiles, 0, i % tiles))
        out_shape[2] = jax.ShapeDtypeStruct((m // seq_len, W_A, seq_len), BF16)
        out_specs.append(pl.BlockSpec((tm // MOBA_BLOCK, 1, W_A), lambda i: (i, 0, 0)))
        out_shape.append(jax.ShapeDtypeStruct((m // MOBA_BLOCK, 1, W_A), F32))
    in_specs = [_row_spec(tm, D_MODEL), _layer_spec(layer, (1, D_MODEL)),
                _layer_spec(layer, (D_MODEL, D_IN)), _layer_spec(layer, (1, W_A)),
                _layer_spec(layer, (1, W_A)), _const_spec((W_A // 2, W_A // 2))]
    args = [x, g, w, gq, gk, e]
    aliases = {}
    if kv_stacks is not None:
        for n, stack in zip(stacked, kv_stacks):
            aliases[len(args)] = n
            in_specs.append(pl.BlockSpec(memory_space=pl.ANY))
            args.append(stack)
    return pl.pallas_call(
        functools.partial(_inproj_body, seq_major=seq_len is not None),
        grid=(m // tm,),
        in_specs=in_specs,
        out_specs=out_specs,
        out_shape=out_shape,
        input_output_aliases=aliases,
        compiler_params=_params("parallel"),
        name="inproj",
    )(*args)


def _outproj_body(oa_ref, or_ref, ga_ref, gb_ref, x_ref, wpa_ref, wpr_ref, wo_ref, o_ref):
    a = _mm(oa_ref[...], wpa_ref[...])
    r = _mm(or_ref[...], wpr_ref[...])
    gate_a = jax.nn.sigmoid(ga_ref[...].astype(F32))
    gate_r = jax.nn.sigmoid(gb_ref[...].astype(F32))
    merged = gate_a * a + gate_r * r
    o_ref[...] = x_ref[...] + _mm(merged.astype(BF16), wo_ref[...])


def _outproj(oa, o_r, ga, gb, x, layer, wpa, wpr, wo, tm):
    m = x.shape[0]
    return pl.pallas_call(
        _outproj_body,
        grid=(m // tm,),
        in_specs=[_row_spec(tm, W_A), _row_spec(tm, W_R), _row_spec(tm, D_MODEL),
                  _row_spec(tm, D_MODEL), _row_spec(tm, D_MODEL),
                  _layer_spec(layer, (W_A, D_MODEL)), _layer_spec(layer, (W_R, D_MODEL)),
                  _layer_spec(layer, (D_MODEL, D_MODEL))],
        out_specs=_row_spec(tm, D_MODEL),
        out_shape=jax.ShapeDtypeStruct((m, D_MODEL), F32),
        compiler_params=_params("parallel"),
        name="outproj",
    )(oa, o_r, ga, gb, x, wpa, wpr, wo)


def _rank_select(gate, lane, n_valid_mask, candidates):
    gate = jnp.where(n_valid_mask, gate, NEG_INF)
    cnt = jnp.zeros(gate.shape, F32)
    for c in candidates:
        gc = gate[:, c:c + 1]
        beats = (gc > gate) | ((gc == gate) & (lane > c))
        cnt = cnt + jnp.where(beats, 1.0, 0.0)
    return jnp.where((cnt < MOBA_TOPK) & n_valid_mask, 1.0, 0.0)


def _moba_tables(s):
    nb = s // MOBA_BLOCK
    pos = jnp.arange(s)
    blk = (pos // MOBA_BLOCK).astype(F32)
    within = (pos % MOBA_BLOCK).astype(F32)
    ind = (pos[:, None] // MOBA_BLOCK == jnp.arange(nb)[None, :]).astype(F32)
    slopes = 2.0 ** (-8.0 * (jnp.arange(N_HEADS_A, dtype=F32) + 1.0) / N_HEADS_A)

    def half(slope):
        cols = [ind, (slope * MOBA_BLOCK * blk)[:, None], (slope * within)[:, None],
                jnp.zeros((s, HEAD_DIM_A - nb - 2), F32)]
        return jnp.concatenate(cols, axis=1)

    pairs = [jnp.concatenate([half(slopes[2 * p + 1]), half(slopes[2 * p])], axis=1).T
             for p in range(N_HEADS_A // 2)]
    return jnp.stack(pairs).astype(BF16)


def _moba_prompt_body(q_ref, k_ref, v_ref, km_ref, tab_ref, o_ref, qa_ref, qb_ref, ka_ref, kb_ref,
                      va_ref, vb_ref, *, nb):
    blk = MOBA_BLOCK
    s_len = nb * blk
    q2 = q_ref[0]
    lane = lax.broadcasted_iota(jnp.int32, (1, LANES), 1)
    lo_half = lane < HEAD_DIM_A

    km = km_ref[0]
    km_parts = []
    for part in (jnp.where(lo_half, km, 0.0), jnp.where(lo_half, 0.0, km)):
        km_parts.extend(_split_bf16(part))
    kms = jnp.concatenate(km_parts, axis=0)
    q_hi, q_lo = _split_bf16(q2)
    r_hi = _nt(kms, q_hi)
    r_lo = _nt(kms, q_lo)
    gates = [r_hi[0:nb] + r_hi[nb:2 * nb] + r_lo[0:nb],
             r_hi[2 * nb:3 * nb] + r_hi[3 * nb:4 * nb] + r_lo[2 * nb:3 * nb]]

    brow = lax.broadcasted_iota(jnp.int32, (nb, s_len), 0)
    own = lax.broadcasted_iota(jnp.int32, (nb, s_len), 1) // blk
    past = brow < own

    def block_bias(gate):
        g = jnp.where(past, gate, NEG_INF)
        cnt = jnp.zeros((nb, s_len), F32)
        for c in range(nb):
            gc = g[c:c + 1, :]
            beats = (gc > g) | ((gc == g) & (brow > c))
            cnt = cnt + jnp.where(beats, 1.0, 0.0)
        allowed = ((cnt < MOBA_TOPK) & past) | (brow == own)
        return jnp.where(allowed, 0.0, NEG_INF)

    ones2 = jnp.where(lax.broadcasted_iota(jnp.int32, (8, s_len), 0) < 2, 1.0, 0.0)
    pad = jnp.zeros((HEAD_DIM_A - nb - 8, s_len), F32)
    ext_t = jnp.concatenate([block_bias(gates[1]), ones2, pad, block_bias(gates[0]), ones2, pad], axis=0)
    ext = ext_t.T
    qa_ref[...] = jnp.where(lo_half, q2, ext).astype(BF16)
    qb_ref[...] = jnp.where(lo_half, ext, q2).astype(BF16)
    k2 = k_ref[0]
    tab = tab_ref[0]
    lo_rows = lax.broadcasted_iota(jnp.int32, (LANES, 1), 0) < HEAD_DIM_A
    ka_ref[...] = jnp.where(lo_rows, k2, tab)
    kb_ref[...] = jnp.where(lo_rows, tab, k2)
    v2 = v_ref[0]
    sum_lane_a, sum_lane_b = HEAD_DIM_A, 0
    va_ref[...] = jnp.where(lo_half, v2, jnp.where(lane == sum_lane_a, 1.0, 0.0).astype(BF16))
    vb_ref[...] = jnp.where(lo_half, jnp.where(lane == sum_lane_b, 1.0, 0.0).astype(BF16), v2)

    row = lax.broadcasted_iota(jnp.int32, (blk, blk), 0)
    col = lax.broadcasted_iota(jnp.int32, (blk, blk), 1)
    causal = col <= row

    heads = ((qa_ref, ka_ref, va_ref, sum_lane_a), (qb_ref, kb_ref, vb_ref, sum_lane_b))
    items = [(j, h) for j in range(nb) for h in range(2)]

    def scores(j, h):
        qx_ref, kx_ref, _, _ = heads[h]
        lo, hi = j * blk, (j + 1) * blk
        qx = qx_ref[lo:hi, :]
        s_own = jnp.where(causal, _mm(qx, kx_ref[:, lo:hi]), NEG_INF)
        s_past = _mm(qx, kx_ref[:, 0:lo]) if j > 0 else None
        return s_own, s_past

    def probs(j, s_own, s_past):
        m = jnp.max(s_own, axis=-1, keepdims=True)
        if j > 0:
            m = jnp.maximum(m, jnp.max(s_past, axis=-1, keepdims=True))
        p_own = jnp.exp(s_own - m).astype(BF16)
        p_past = jnp.exp(s_past - m).astype(BF16) if j > 0 else None
        return p_own, p_past

    def weighted(j, h, p_own, p_past):
        _, _, vx_ref, sum_lane = heads[h]
        lo, hi = j * blk, (j + 1) * blk
        acc = _mm(p_own, vx_ref[lo:hi, :])
        if j > 0:
            acc = acc + _mm(p_past, vx_ref[0:lo, :])
        return acc / acc[:, sum_lane:sum_lane + 1]

    outs = []

    def finish(j, h, p):
        outs.append(weighted(j, h, *p))
        if h == 1:
            lo, hi = j * blk, (j + 1) * blk
            o_ref[0, lo:hi, :] = jnp.where(lo_half, outs[0], outs[1]).astype(o_ref.dtype)
            outs.clear()

    ahead = scores(*items[0])
    behind = None
    for k, (j, h) in enumerate(items):
        current = ahead
        if k + 1 < len(items):
            ahead = scores(*items[k + 1])
        p = probs(j, *current)
        if behind:
            finish(*behind)
        behind = (j, h, p)
    finish(*behind)


def _moba_prompt(q, kb, vb, km, tables):
    b, s, _ = q.shape
    nb = s // MOBA_BLOCK
    assert nb == 8, "block-indicator columns are laid out for 8 MoBA blocks"
    npair = N_HEADS_A // 2
    seq_spec = pl.BlockSpec((1, s, LANES), lambda bi, p: (bi, 0, p))
    return pl.pallas_call(
        functools.partial(_moba_prompt_body, nb=nb),
        grid=(b, npair),
        in_specs=[seq_spec, pl.BlockSpec((1, LANES, s), lambda bi, p: (bi, p, 0)), seq_spec,
                  pl.BlockSpec((1, nb, LANES), lambda bi, p: (bi, 0, p)),
                  pl.BlockSpec((1, LANES, s), lambda bi, p: (p, 0, 0))],
        out_specs=seq_spec,
        out_shape=jax.ShapeDtypeStruct((b, s, W_A), BF16),
        scratch_shapes=[pltpu.VMEM((s, LANES), BF16)] * 2 + [pltpu.VMEM((LANES, s), BF16)] * 2
        + [pltpu.VMEM((s, LANES), BF16)] * 2,
        compiler_params=_params("parallel", "parallel"),
        name="moba_prompt",
    )(q, kb, vb, km, tables)


def _moba_sample_stages(pt_ref, q_ref, kn_ref, vn_ref, kt_hbm, vt_hbm, o_ref, buf, sem, s_ref,
                        p_ref, *, layer, n_real, past_len):
    b = pl.program_id(0)
    n_seq = pl.num_programs(0)
    n_pages = s_ref.shape[0]
    ring = PAGE_SLOTS
    rows = N_HEADS_A * n_real

    def page_copy(hbm, seq, pg):
        slot = pg % ring
        return pltpu.make_async_copy(hbm.at[layer, pt_ref[seq, pg]], buf.at[slot], sem.at[slot])

    @pl.when(b == 0)
    def _():
        for pg in range(ring):
            page_copy(kt_hbm, b, pg).start()

    rowi = lax.broadcasted_iota(jnp.int32, (rows, W_A), 0)
    lanei = lax.broadcasted_iota(jnp.int32, (rows, W_A), 1)
    head_mask = (lanei >> 6) == (rowi & (N_HEADS_A - 1))
    q = q_ref[0]
    q_rep = jnp.concatenate(
        [jnp.broadcast_to(q[t:t + 1, :], (N_HEADS_A, W_A)) for t in range(n_real)], axis=0)
    q_hi, q_lo = _split_bf16(jnp.where(head_mask, q_rep, 0.0))
    q2 = jnp.concatenate([q_hi, q_lo], axis=0)

    def raw_scores(keys_t=None, keys=None):
        s2 = _mm(q2, keys_t) if keys_t is not None else _nt(q2, keys)
        return s2[:rows] + s2[rows:]

    lane = lax.broadcasted_iota(jnp.int32, (1, LANES), 1)

    def start_group(hbm, seq, first):
        for pg in range(first, first + PAGE_GROUP):
            page_copy(hbm, seq, pg).start()

    def wait_group(hbm, first):
        for pg in range(first, first + PAGE_GROUP):
            page_copy(hbm, b, pg).wait()

    st = {"gsum": jnp.zeros((rows, LANES), F32)}

    def k_group(first):
        gsum = st["gsum"]
        for pg in range(first, first + PAGE_GROUP):
            s = raw_scores(keys_t=buf[pg % ring].astype(BF16))
            s_ref[pg] = s
            gsum = jnp.where(lane == pg, jnp.sum(s, axis=-1, keepdims=True), gsum)
        st["gsum"] = gsum

    def k_refill(first):
        if first + ring < n_pages:
            start_group(kt_hbm, b, first + ring)
        else:
            start_group(vt_hbm, b, first + ring - n_pages)

    def select_and_softmax():
        gsum = st["gsum"]
        gate = (gsum + pltpu.roll(gsum, LANES - 1, 1)) * (1.0 / MOBA_BLOCK)
        valid = ((lane & 1) == 0) & (lane < n_pages)
        sel = _rank_select(gate, lane, valid, range(0, n_pages, 2))

        rcol = lax.broadcasted_iota(jnp.int32, (rows, 1), 0)
        slope = jnp.exp2(-((rcol & (N_HEADS_A - 1)) + 1).astype(F32))
        qpos = past_len + (rcol >> 3)

        m_lanes = jnp.full((rows, LANES), NEG_INF, F32)
        for blk in range(n_pages // 2):
            chosen = sel[:, 2 * blk:2 * blk + 1] > 0.5
            for pg in (2 * blk, 2 * blk + 1):
                kpos = pg * LANES + lane
                s = s_ref[pg] - slope * (qpos - kpos).astype(F32)
                s = jnp.where(chosen, s, NEG_INF)
                s_ref[pg] = s
                m_lanes = jnp.maximum(m_lanes, s)
        tnew = lax.broadcasted_iota(jnp.int32, (1, SAMPLE_ROWS), 1)
        dist = (rcol >> 3) - tnew
        s_own = raw_scores(keys=kn_ref[0]) - slope * dist.astype(F32)
        s_own = jnp.where(dist >= 0, s_own, NEG_INF)
        m = jnp.maximum(jnp.max(m_lanes, axis=-1, keepdims=True),
                        jnp.max(s_own, axis=-1, keepdims=True))

        l_lanes = jnp.zeros((rows, LANES), F32)
        for pg in range(n_pages):
            p = jnp.exp(s_ref[pg] - m)
            l_lanes = l_lanes + p
            p_ref[pg] = p.astype(BF16)
        p_own = jnp.exp(s_own - m)
        st["l"] = jnp.sum(l_lanes, axis=-1, keepdims=True) + jnp.sum(p_own, axis=-1, keepdims=True)
        st["acc"] = _mm(p_own.astype(BF16), vn_ref[0])

    def v_group(first):
        acc = st["acc"]
        for pg in range(first, first + PAGE_GROUP):
            acc = acc + _nt(p_ref[pg], buf[pg % ring].astype(BF16))
        st["acc"] = acc

    def v_refill(first):
        if first + ring < n_pages:
            start_group(vt_hbm, b, first + ring)
        else:
            @pl.when(b + 1 < n_seq)
            def _():
                start_group(kt_hbm, b + 1, first + ring - n_pages)

    def finish():
        o = jnp.where(head_mask, st["acc"] / st["l"], 0.0)
        out_rows = [jnp.sum(o[N_HEADS_A * t:N_HEADS_A * (t + 1), :], axis=0, keepdims=True)
                    for t in range(n_real)]
        out_rows.append(jnp.zeros((SAMPLE_ROWS - n_real, W_A), F32))
        o_ref[0] = jnp.concatenate(out_rows, axis=0).astype(o_ref.dtype)

    part = functools.partial
    nothing = lambda: None
    groups = range(0, n_pages, PAGE_GROUP)
    return ([(part(wait_group, kt_hbm, f), part(k_group, f), part(k_refill, f)) for f in groups]
            + [(nothing, select_and_softmax, nothing)]
            + [(part(wait_group, vt_hbm, f), part(v_group, f), part(v_refill, f)) for f in groups]
            + [(nothing, finish, nothing)])


def _ffn_moba_sample_body(pt_ref, x_ref, g_ref, w1_ref, w3_ref, w2_ref, q_ref, kn_ref, vn_ref,
                          kt_hbm, vt_hbm, y_ref, o_ref, buf, sem, s_ref, p_ref,
                          *, layer, n_real, past_len):
    stages = _moba_sample_stages(pt_ref, q_ref, kn_ref, vn_ref, kt_hbm, vt_hbm, o_ref, buf, sem,
                                 s_ref, p_ref, layer=layer, n_real=n_real, past_len=past_len)
    x = x_ref[...]
    h = _rms(x, g_ref[...]).astype(BF16)
    n_chunks = D_FF // FFN_CHUNK
    assert len(stages) <= n_chunks
    acc = jnp.zeros(x.shape, F32)
    for c in range(n_chunks):
        cols = slice(c * FFN_CHUNK, (c + 1) * FFN_CHUNK)
        a = _mm(h, w1_ref[:, cols])
        gate = _mm(h, w3_ref[:, cols])
        act = (a * jax.nn.sigmoid(a) * gate).astype(BF16)
        acc = acc + _mm(act, w2_ref[cols, :])
        if c < len(stages):
            for part in stages[c]:
                part()
    y_ref[...] = x + 0.5 * acc


def _ffn_moba_sample(x, layer, g, w1, w3, w2, tm, page_table, q, kn, vn, cache_kt, cache_vt, n_real):
    m = x.shape[0]
    bd, n_pages = page_table.shape
    page = cache_kt.shape[-1]
    assert m // tm == bd, "one ffn row tile per sample sequence"
    assert page == LANES and MOBA_BLOCK == 2 * page
    assert PAGE_SLOTS <= n_pages <= LANES and n_pages % PAGE_GROUP == 0 == PAGE_SLOTS % PAGE_GROUP
    rows = N_HEADS_A * n_real

    def layer_spec(shape):
        return pl.BlockSpec((None,) + shape, lambda i, pt: (layer, 0, 0), pipeline_mode=pl.Buffered(1))

    row_spec = pl.BlockSpec((tm, D_MODEL), lambda i, pt: (i, 0))
    tok_spec = pl.BlockSpec((1, SAMPLE_ROWS, W_A), lambda i, pt: (i, 0, 0))
    hbm_spec = pl.BlockSpec(memory_space=pl.ANY)
    grid_spec = pltpu.PrefetchScalarGridSpec(
        num_scalar_prefetch=1,
        grid=(bd,),
        in_specs=[row_spec, layer_spec((1, D_MODEL)), layer_spec((D_MODEL, D_FF)),
                  layer_spec((D_MODEL, D_FF)), layer_spec((D_FF, D_MODEL)),
                  tok_spec, tok_spec, tok_spec, hbm_spec, hbm_spec],
        out_specs=[row_spec, tok_spec],
        scratch_shapes=[pltpu.VMEM((PAGE_SLOTS, W_A, page), F32),
                        pltpu.SemaphoreType.DMA((PAGE_SLOTS,)),
                        pltpu.VMEM((n_pages, rows, LANES), F32),
                        pltpu.VMEM((n_pages, rows, LANES), BF16)],
    )
    return pl.pallas_call(
        functools.partial(_ffn_moba_sample_body, layer=layer, n_real=n_real,
                          past_len=n_pages * page),
        grid_spec=grid_spec,
        out_shape=[jax.ShapeDtypeStruct((m, D_MODEL), F32),
                   jax.ShapeDtypeStruct((bd, SAMPLE_ROWS, W_A), BF16)],
        compiler_params=_params("arbitrary"),
        name="ffn_moba_sample",
    )(page_table, x, g, w1, w3, w2, q, kn, vn, cache_kt, cache_vt)


def _log_decay_rows():
    log_decay = jnp.log(1.0 - 2.0 ** (-5.0 - jnp.arange(N_HEADS_R, dtype=F32)))
    return jnp.broadcast_to(log_decay.reshape(N_HEADS_R, 1, 1), (N_HEADS_R, 1, LANES))


def _retention_body(lg_ref, q_ref, k_ref, v_ref, gr_ref, gain_ref, bias_ref, s0_ref,
                    o_ref, s_out_ref, *, chunk, n_real, n_chunks, heads):
    hd = KEY_DIM_R
    ii = lax.broadcasted_iota(jnp.int32, (chunk, chunk), 0)
    jj = lax.broadcasted_iota(jnp.int32, (chunk, chunk), 1)
    diff = (ii - jj).astype(F32)
    icol = lax.broadcasted_iota(jnp.int32, (chunk, 1), 0).astype(F32)
    for h in range(heads):
        cols = slice(h * hd, (h + 1) * hd)
        lg = lg_ref[h][:, 0:1]
        dmat = jnp.where(diff >= 0, jnp.exp(lg * jnp.maximum(diff, 0.0)), 0.0)
        qdec = jnp.exp(lg * (icol + 1.0))
        kdec = jnp.where(icol < n_real, jnp.exp(lg * (n_real - 1.0 - icol)), 0.0)
        chunk_decay = jnp.exp(lg * float(n_real))
        gain = gain_ref[:, cols]
        bias = bias_ref[:, cols]
        state = s0_ref[0, h]
        for c in range(n_chunks):
            rows = slice(c * chunk, (c + 1) * chunk)
            qc = q_ref[0, rows, cols]
            kc = k_ref[0, rows, cols]
            vc = v_ref[0, rows, cols]
            inner = _nt(qc, kc) * dmat
            o = _mm(inner.astype(BF16), vc) + _mm(qc, state.astype(BF16)) * qdec
            kd = (kc.astype(F32) * kdec).astype(BF16)
            state = chunk_decay * state + _tn(kd, vc)
            mu = jnp.mean(o, axis=-1, keepdims=True)
            d = o - mu
            var = jnp.mean(d * d, axis=-1, keepdims=True)
            y = d * lax.rsqrt(var + EPS) * gain + bias
            gr = gr_ref[0, rows, cols].astype(F32)
            o_ref[0, rows, cols] = (y * (gr * jax.nn.sigmoid(gr))).astype(o_ref.dtype)
        s_out_ref[0, h] = state


def _retention(lg, q, k, v, gr, gain, bias, layer, state0, state_offset, chunk, n_real, heads):
    b, s, _ = q.shape
    hd = KEY_DIM_R
    width = heads * hd
    seq_spec = pl.BlockSpec((1, s, width), lambda bi, g: (bi, 0, g))
    vec_spec = pl.BlockSpec((None, 1, width), lambda bi, g: (layer, 0, g))
    st_spec = pl.BlockSpec((1, heads, hd, hd), lambda bi, g: (bi, g, 0, 0))
    st_in_spec = pl.BlockSpec((1, heads, hd, hd), lambda bi, g: (state_offset + bi, g, 0, 0))
    return pl.pallas_call(
        functools.partial(_retention_body, chunk=chunk, n_real=n_real, n_chunks=s // chunk,
                          heads=heads),
        grid=(b, N_HEADS_R // heads),
        in_specs=[pl.BlockSpec((heads, 1, LANES), lambda bi, g: (g, 0, 0)),
                  seq_spec, seq_spec, seq_spec, seq_spec, vec_spec, vec_spec, st_in_spec],
        out_specs=[seq_spec, st_spec],
        out_shape=[jax.ShapeDtypeStruct((b, s, W_R), BF16),
                   jax.ShapeDtypeStruct((b, N_HEADS_R, hd, hd), F32)],
        compiler_params=_params("parallel", "parallel"),
        name="retention",
    )(lg, q, k, v, gr, gain, bias, state0)


def kernel(x_prompt, x_sample, cache_k, cache_v, state_ret, page_table, g_ffn1, w1_ffn1, w3_ffn1,
           w2_ffn1, g_mix, w_in, g_q, g_k, gn_gain, gn_bias, w_pa, w_pr, w_o, g_ffn2, w1_ffn2,
           w3_ffn2, w2_ffn2):
    b, s, _ = x_prompt.shape
    bd, sd, _ = x_sample.shape
    depth = w_in.shape[0]
    n_pool, page = cache_k.shape[1], cache_k.shape[2]
    mp, ms = b * s, bd * SAMPLE_ROWS
    tm_p, tm_s = 512, ms

    xp = x_prompt.reshape(mp, D_MODEL)
    xs = jnp.pad(x_sample, ((0, 0), (0, SAMPLE_ROWS - sd), (0, 0))).reshape(ms, D_MODEL)

    cache_kt = jnp.transpose(cache_k, (0, 1, 3, 4, 2)).reshape(depth, n_pool, W_A, page)
    cache_vt = jnp.transpose(cache_v, (0, 1, 3, 4, 2)).reshape(depth, n_pool, W_A, page)

    tables = _moba_tables(s)
    lg = _log_decay_rows()
    head_ones = jnp.kron(jnp.eye(N_HEADS_A // 2, dtype=F32),
                         jnp.ones((HEAD_DIM_A, HEAD_DIM_A), F32)).astype(BF16)
    zero_state = jnp.zeros((b, N_HEADS_R, KEY_DIM_R, VAL_DIM_R), F32)
    states_in = state_ret.reshape(depth * bd, N_HEADS_R, KEY_DIM_R, VAL_DIM_R)

    vec = lambda a: a.reshape(depth, 1, -1)
    g1, gm, g2, gain, bias = vec(g_ffn1), vec(g_mix), vec(g_ffn2), vec(gn_gain), vec(gn_bias)
    gq, gk = vec(jnp.tile(g_q, (1, N_HEADS_A))), vec(jnp.tile(g_k, (1, N_HEADS_A)))
    w1a, w3a, w2a = w1_ffn1.astype(BF16), w3_ffn1.astype(BF16), w2_ffn1.astype(BF16)
    w1b, w3b, w2b = w1_ffn2.astype(BF16), w3_ffn2.astype(BF16), w2_ffn2.astype(BF16)
    win, wpa, wpr, wo = w_in.astype(BF16), w_pa.astype(BF16), w_pr.astype(BF16), w_o.astype(BF16)

    kv_p = kv_s = None
    sp_l, ss_l = [], []
    for l in range(depth):
        xp = _ffn(xp, l, g1, w1a, w3a, w2a, tm_p)
        q, kf, kb, vf, vb, qr, kr, vr, gr, ga, gb, km = _inproj(
            xp, l, depth, gm, win, gq, gk, head_ones, tm_p, kv_p, seq_len=s)
        kv_p = (kf, vf)
        r3 = lambda a: a.reshape(b, s, a.shape[-1])
        oa = _moba_prompt(r3(q), kb, r3(vb), km.reshape(b, s // MOBA_BLOCK, W_A), tables)
        o_r, st = _retention(lg, r3(qr), r3(kr), r3(vr), r3(gr), gain, bias, l, zero_state, 0,
                             RET_CHUNK_PROMPT, RET_CHUNK_PROMPT, RET_HEADS_PROMPT)
        xp = _outproj(oa.reshape(mp, W_A), o_r.reshape(mp, W_R), ga, gb, xp, l, wpa, wpr, wo, tm_p)
        sp_l.append(st)

        xs = _ffn_few_rows(xs, l, g1, w1a, w3a, w2a)
        q, kf, kb, vf, vb, qr, kr, vr, gr, ga, gb = _inproj(
            xs, l, depth, gm, win, gq, gk, head_ones, tm_s, kv_s)
        kv_s = (kf, vf)
        r3 = lambda a: a.reshape(bd, SAMPLE_ROWS, a.shape[-1])
        xp, oa = _ffn_moba_sample(xp, l, g2, w1b, w3b, w2b, tm_p, page_table, r3(q), r3(kb), r3(vb),
                                  cache_kt, cache_vt, sd)
        o_r, st = _retention(lg, r3(qr), r3(kr), r3(vr), r3(gr), gain, bias, l, states_in, l * bd,
                             SAMPLE_ROWS, sd, N_HEADS_R)
        xs = _outproj(oa.reshape(ms, W_A), o_r.reshape(ms, W_R), ga, gb, xs, l, wpa, wpr, wo, tm_s)
        xs = _ffn_few_rows(xs, l, g2, w1b, w3b, w2b)
        ss_l.append(st)

    y_prompt = xp.reshape(b, s, D_MODEL)
    y_sample = xs.reshape(bd, SAMPLE_ROWS, D_MODEL)[:, :sd]
    seq_minor = lambda a: a.reshape(depth, b, N_HEADS_A, HEAD_DIM_A, s).transpose(0, 1, 4, 2, 3)
    k_prompt, v_prompt = seq_minor(kv_p[0]), seq_minor(kv_p[1])
    heads5 = lambda a: a.reshape(depth, bd, SAMPLE_ROWS, N_HEADS_A, HEAD_DIM_A)[:, :, :sd]
    k_sample, v_sample = heads5(kv_s[0]), heads5(kv_s[1])
    return (y_prompt, y_sample, k_prompt, v_prompt, jnp.stack(sp_l),
            k_sample, v_sample, jnp.stack(ss_l))
```

```python
import functools

import jax
import jax.numpy as jnp
import numpy as np
from jax import lax
from jax.experimental import pallas as pl
from jax.experimental.pallas import tpu as pltpu

F32 = jnp.float32
BF16 = jnp.bfloat16

D_MODEL = 1024
D_FF = 2816
N_HEADS_A = 8
HEAD_DIM_A = 64
MOBA_BLOCK = 256
MOBA_TOPK = 3
N_HEADS_R = 4
KEY_DIM_R = 128
VAL_DIM_R = 128
EPS = 1e-6
NEG_INF = -1e30
W_A = N_HEADS_A * HEAD_DIM_A
W_R = N_HEADS_R * KEY_DIM_R

LANES = 128
VMEM_LIMIT_BYTES = 56 * 1024 * 1024
SAMPLE_ROWS = 16
RET_CHUNK_PROMPT = 256
RET_HEADS_PROMPT = 2
PAGE_SLOTS = 32
PAGE_GROUP = 16
FFN_CHUNK = 256


def _nt(a, b):
    return lax.dot_general(a, b, (((1,), (1,)), ((), ())), preferred_element_type=F32)


def _tn(a, b):
    return lax.dot_general(a, b, (((0,), (0,)), ((), ())), preferred_element_type=F32)


def _mm(a, b):
    return jnp.dot(a, b, preferred_element_type=F32)


def _rms(x, g):
    return x * lax.rsqrt(jnp.mean(x * x, axis=-1, keepdims=True) + EPS) * g


def _split_bf16(x):
    hi = x.astype(BF16)
    lo = (x - hi.astype(F32)).astype(BF16)
    return hi, lo


def _params(*sem):
    return pltpu.CompilerParams(dimension_semantics=sem, vmem_limit_bytes=VMEM_LIMIT_BYTES)


def _row_spec(tm, cols):
    return pl.BlockSpec((tm, cols), lambda i: (i, 0))


def _const_spec(shape):
    return pl.BlockSpec(shape, lambda i: (0,) * len(shape), pipeline_mode=pl.Buffered(1))


def _layer_spec(layer, shape):
    return pl.BlockSpec((None,) + shape, lambda i: (layer, 0, 0), pipeline_mode=pl.Buffered(1))


def _ffn_body(x_ref, g_ref, w1_ref, w3_ref, w2_ref, o_ref):
    x = x_ref[...]
    h = _rms(x, g_ref[...]).astype(BF16)
    a = _mm(h, w1_ref[...])
    b = _mm(h, w3_ref[...])
    act = (a * jax.nn.sigmoid(a) * b).astype(BF16)
    o_ref[...] = x + 0.5 * _mm(act, w2_ref[...])


def _ffn(x, layer, g, w1, w3, w2, tm):
    m = x.shape[0]
    return pl.pallas_call(
        _ffn_body,
        grid=(m // tm,),
        in_specs=[_row_spec(tm, D_MODEL), _layer_spec(layer, (1, D_MODEL)),
                  _layer_spec(layer, (D_MODEL, D_FF)), _layer_spec(layer, (D_MODEL, D_FF)),
                  _layer_spec(layer, (D_FF, D_MODEL))],
        out_specs=_row_spec(tm, D_MODEL),
        out_shape=jax.ShapeDtypeStruct((m, D_MODEL), F32),
        compiler_params=_params("parallel"),
        name="ffn",
    )(x, g, w1, w3, w2)


def _ffn_few_rows_body(x_ref, g_ref, w1_ref, w3_ref, w2_ref, o_ref, h_ref, acc_ref):
    c = pl.program_id(0)

    @pl.when(c == 0)
    def _():
        h_ref[...] = _rms(x_ref[...], g_ref[...]).astype(BF16)
        acc_ref[...] = jnp.zeros_like(acc_ref)

    h = h_ref[...]
    a = _mm(h, w1_ref[...])
    act = (a * jax.nn.sigmoid(a) * _mm(h, w3_ref[...])).astype(BF16)
    acc_ref[...] += _mm(act, w2_ref[...])

    @pl.when(c == pl.num_programs(0) - 1)
    def _():
        o_ref[...] = x_ref[...] + 0.5 * acc_ref[...]


def _ffn_few_rows(x, layer, g, w1, w3, w2):
    m = x.shape[0]
    whole = pl.BlockSpec((m, D_MODEL), lambda c: (0, 0))
    return pl.pallas_call(
        _ffn_few_rows_body,
        grid=(D_FF // FFN_CHUNK,),
        in_specs=[whole, pl.BlockSpec((None, 1, D_MODEL), lambda c: (layer, 0, 0)),
                  pl.BlockSpec((None, D_MODEL, FFN_CHUNK), lambda c: (layer, 0, c)),
                  pl.BlockSpec((None, D_MODEL, FFN_CHUNK), lambda c: (layer, 0, c)),
                  pl.BlockSpec((None, FFN_CHUNK, D_MODEL), lambda c: (layer, c, 0))],
        out_specs=whole,
        out_shape=jax.ShapeDtypeStruct((m, D_MODEL), F32),
        scratch_shapes=[pltpu.VMEM((m, D_MODEL), BF16), pltpu.VMEM((m, D_MODEL), F32)],
        compiler_params=_params("arbitrary"),
        name="ffn_few_rows",
    )(x, g, w1, w3, w2)


_IN_COLS = (W_A, W_A, W_A, W_R, W_R, W_R, W_R, D_MODEL, D_MODEL)
_IN_OFFS = tuple(sum(_IN_COLS[:i]) for i in range(len(_IN_COLS) + 1))
D_IN = _IN_OFFS[-1]


def _inproj_body(x_ref, g_ref, w_ref, gq_ref, gk_ref, e_ref, *rest, seq_major):
    n_out = 12 if seq_major else 11
    q_o, kf_o, kb_o, vf_o, vb_o, qr_o, kr_o, vr_o, gr_o, ga_o, gb_o = rest[-n_out:][:11]
    h = _rms(x_ref[...], g_ref[...]).astype(BF16)

    def proj(i):
        return _mm(h, w_ref[:, _IN_OFFS[i]:_IN_OFFS[i + 1]])

    def head_rms(z, gain):
        zz = (z * z).astype(BF16)
        half = W_A // 2
        ss = jnp.concatenate([_mm(zz[:, :half], e_ref[...]), _mm(zz[:, half:], e_ref[...])], axis=1)
        return z * lax.rsqrt(ss * (1.0 / HEAD_DIM_A) + EPS) * gain

    q_o[...] = head_rms(proj(0), gq_ref[...]) * (HEAD_DIM_A ** -0.5)
    k = head_rms(proj(1), gk_ref[...])
    v = proj(2)
    vb_o[...] = v.astype(BF16)
    if seq_major:
        kt = k.T
        kf_o[...] = kt
        kb_o[...] = kt.astype(BF16)
        vf_o[...] = v.T
        km_o = rest[-1]
        for r in range(km_o.shape[0]):
            km_o[r] = jnp.mean(k[r * MOBA_BLOCK:(r + 1) * MOBA_BLOCK, :], axis=0, keepdims=True)
    else:
        kf_o[...] = k
        kb_o[...] = k.astype(BF16)
        vf_o[...] = v
    qr_o[...] = proj(3).astype(BF16)
    kr_o[...] = (proj(4) * (KEY_DIM_R ** -0.5)).astype(BF16)
    vr_o[...] = proj(5).astype(BF16)
    gr_o[...] = proj(6).astype(BF16)
    ga_o[...] = proj(7).astype(BF16)
    gb_o[...] = proj(8).astype(BF16)


def _inproj(x, layer, depth, g, w, gq, gk, e, tm, kv_stacks=None, seq_len=None):
    m = x.shape[0]
    outs = [(W_A, F32), (W_A, F32), (W_A, BF16), (W_A, F32), (W_A, BF16),
            (W_R, BF16), (W_R, BF16), (W_R, BF16), (W_R, BF16), (D_MODEL, BF16), (D_MODEL, BF16)]
    stacked = (1, 3)
    out_specs = [_row_spec(tm, c) for c, _ in outs]
    out_shape = [jax.ShapeDtypeStruct((m, c), dt) for c, dt in outs]
    for n in stacked:
        if seq_len is None:
            out_specs[n] = pl.BlockSpec((None, tm, W_A), lambda i: (layer, i, 0))
            out_shape[n] = jax.ShapeDtypeStruct((depth, m, W_A), F32)
        else:
            tiles = seq_len // tm
            out_specs[n] = pl.BlockSpec((None, None, W_A, tm),
                                        lambda i: (layer, i // tiles, 0, i % tiles))
            out_shape[n] = jax.ShapeDtypeStruct((depth, m // seq_len, W_A, seq_len), F32)
    if seq_len is not None:
        out_specs[2] = pl.BlockSpec((None, W_A, tm), lambda i: (i // tiles, 0, i % tiles))
        out_shape[2] = jax.ShapeDtypeStruct((m // seq_len, W_A, seq_len), BF16)
        out_specs.append(pl.BlockSpec((tm // MOBA_BLOCK, 1, W_A), lambda i: (i, 0, 0)))
        out_shape.append(jax.ShapeDtypeStruct((m // MOBA_BLOCK, 1, W_A), F32))
    in_specs = [_row_spec(tm, D_MODEL), _layer_spec(layer, (1, D_MODEL)),
                _layer_spec(layer, (D_MODEL, D_IN)), _layer_spec(layer, (1, W_A)),
                _layer_spec(layer, (1, W_A)), _const_spec((W_A // 2, W_A // 2))]
    args = [x, g, w, gq, gk, e]
    aliases = {}
    if kv_stacks is not None:
        for n, stack in zip(stacked, kv_stacks):
            aliases[len(args)] = n
            in_specs.append(pl.BlockSpec(memory_space=pl.ANY))
            args.append(stack)
    return pl.pallas_call(
        functools.partial(_inproj_body, seq_major=seq_len is not None),
        grid=(m // tm,),
        in_specs=in_specs,
        out_specs=out_specs,
        out_shape=out_shape,
        input_output_aliases=aliases,
        compiler_params=_params("parallel"),
        name="inproj",
    )(*args)


def _outproj_body(oa_ref, or_ref, ga_ref, gb_ref, x_ref, wpa_ref, wpr_ref, wo_ref, o_ref):
    a = _mm(oa_ref[...], wpa_ref[...])
    r = _mm(or_ref[...], wpr_ref[...])
    gate_a = jax.nn.sigmoid(ga_ref[...].astype(F32))
    gate_r = jax.nn.sigmoid(gb_ref[...].astype(F32))
    merged = gate_a * a + gate_r * r
    o_ref[...] = x_ref[...] + _mm(merged.astype(BF16), wo_ref[...])


def _outproj(oa, o_r, ga, gb, x, layer, wpa, wpr, wo, tm):
    m = x.shape[0]
    return pl.pallas_call(
        _outproj_body,
        grid=(m // tm,),
        in_specs=[_row_spec(tm, W_A), _row_spec(tm, W_R), _row_spec(tm, D_MODEL),
                  _row_spec(tm, D_MODEL), _row_spec(tm, D_MODEL),
                  _layer_spec(layer, (W_A, D_MODEL)), _layer_spec(layer, (W_R, D_MODEL)),
                  _layer_spec(layer, (D_MODEL, D_MODEL))],
        out_specs=_row_spec(tm, D_MODEL),
        out_shape=jax.ShapeDtypeStruct((m, D_MODEL), F32),
        compiler_params=_params("parallel"),
        name="outproj",
    )(oa, o_r, ga, gb, x, wpa, wpr, wo)


def _rank_select(gate, lane, n_valid_mask, candidates):
    gate = jnp.where(n_valid_mask, gate, NEG_INF)
    cnt = jnp.zeros(gate.shape, F32)
    for c in candidates:
        gc = gate[:, c:c + 1]
        beats = (gc > gate) | ((gc == gate) & (lane > c))
        cnt = cnt + jnp.where(beats, 1.0, 0.0)
    return jnp.where((cnt < MOBA_TOPK) & n_valid_mask, 1.0, 0.0)


def _moba_tables(s):
    nb = s // MOBA_BLOCK
    pos = np.arange(s)
    blk = (pos // MOBA_BLOCK).astype(np.float32)
    within = (pos % MOBA_BLOCK).astype(np.float32)
    ind = (pos[:, None] // MOBA_BLOCK == np.arange(nb)[None, :]).astype(np.float32)
    slopes = 2.0 ** (-8.0 * (np.arange(N_HEADS_A, dtype=np.float32) + 1.0) / N_HEADS_A)

    def half(slope):
        cols = [ind, (slope * MOBA_BLOCK * blk)[:, None], (slope * within)[:, None],
                np.zeros((s, HEAD_DIM_A - nb - 2), np.float32)]
        return np.concatenate(cols, axis=1)

    pairs = [np.concatenate([half(slopes[2 * p + 1]), half(slopes[2 * p])], axis=1).T
             for p in range(N_HEADS_A // 2)]
    return np.stack(pairs).astype(BF16)


def _moba_prompt_body(q_ref, k_ref, v_ref, km_ref, tab_ref, o_ref, qa_ref, qb_ref, ka_ref, kb_ref,
                      va_ref, vb_ref, *, nb):
    blk = MOBA_BLOCK
    s_len = nb * blk
    q2 = q_ref[0]
    lane = lax.broadcasted_iota(jnp.int32, (1, LANES), 1)
    lo_half = lane < HEAD_DIM_A

    km = km_ref[0]
    km_parts = []
    for part in (jnp.where(lo_half, km, 0.0), jnp.where(lo_half, 0.0, km)):
        km_parts.extend(_split_bf16(part))
    kms = jnp.concatenate(km_parts, axis=0)
    q_hi, q_lo = _split_bf16(q2)
    r_hi = _nt(kms, q_hi)
    r_lo = _nt(kms, q_lo)
    gates = [r_hi[0:nb] + r_hi[nb:2 * nb] + r_lo[0:nb],
             r_hi[2 * nb:3 * nb] + r_hi[3 * nb:4 * nb] + r_lo[2 * nb:3 * nb]]

    brow = lax.broadcasted_iota(jnp.int32, (nb, s_len), 0)
    own = lax.broadcasted_iota(jnp.int32, (nb, s_len), 1) // blk
    past = brow < own

    def block_bias(gate):
        g = jnp.where(past, gate, NEG_INF)
        cnt = jnp.zeros((nb, s_len), F32)
        for c in range(nb):
            gc = g[c:c + 1, :]
            beats = (gc > g) | ((gc == g) & (brow > c))
            cnt = cnt + jnp.where(beats, 1.0, 0.0)
        allowed = ((cnt < MOBA_TOPK) & past) | (brow == own)
        return jnp.where(allowed, 0.0, NEG_INF)

    ones2 = jnp.where(lax.broadcasted_iota(jnp.int32, (8, s_len), 0) < 2, 1.0, 0.0)
    pad = jnp.zeros((HEAD_DIM_A - nb - 8, s_len), F32)
    ext_t = jnp.concatenate([block_bias(gates[1]), ones2, pad, block_bias(gates[0]), ones2, pad], axis=0)
    ext = ext_t.T
    qa_ref[...] = jnp.where(lo_half, q2, ext).astype(BF16)
    qb_ref[...] = jnp.where(lo_half, ext, q2).astype(BF16)
    k2 = k_ref[0]
    tab = tab_ref[0]
    lo_rows = lax.broadcasted_iota(jnp.int32, (LANES, 1), 0) < HEAD_DIM_A
    ka_ref[...] = jnp.where(lo_rows, k2, tab)
    kb_ref[...] = jnp.where(lo_rows, tab, k2)
    v2 = v_ref[0]
    sum_lane_a, sum_lane_b = HEAD_DIM_A, 0
    va_ref[...] = jnp.where(lo_half, v2, jnp.where(lane == sum_lane_a, 1.0, 0.0).astype(BF16))
    vb_ref[...] = jnp.where(lo_half, jnp.where(lane == sum_lane_b, 1.0, 0.0).astype(BF16), v2)

    row = lax.broadcasted_iota(jnp.int32, (blk, blk), 0)
    col = lax.broadcasted_iota(jnp.int32, (blk, blk), 1)
    causal = col <= row

    heads = ((qa_ref, ka_ref, va_ref, sum_lane_a), (qb_ref, kb_ref, vb_ref, sum_lane_b))
    items = [(j, h) for j in range(nb) for h in range(2)]

    def scores(j, h):
        qx_ref, kx_ref, _, _ = heads[h]
        lo, hi = j * blk, (j + 1) * blk
        qx = qx_ref[lo:hi, :]
        s_own = jnp.where(causal, _mm(qx, kx_ref[:, lo:hi]), NEG_INF)
        s_past = _mm(qx, kx_ref[:, 0:lo]) if j > 0 else None
        return s_own, s_past

    def probs(j, s_own, s_past):
        m = jnp.max(s_own, axis=-1, keepdims=True)
        if j > 0:
            m = jnp.maximum(m, jnp.max(s_past, axis=-1, keepdims=True))
        p_own = jnp.exp(s_own - m).astype(BF16)
        p_past = jnp.exp(s_past - m).astype(BF16) if j > 0 else None
        return p_own, p_past

    def weighted(j, h, p_own, p_past):
        _, _, vx_ref, sum_lane = heads[h]
        lo, hi = j * blk, (j + 1) * blk
        acc = _mm(p_own, vx_ref[lo:hi, :])
        if j > 0:
            acc = acc + _mm(p_past, vx_ref[0:lo, :])
        return acc / acc[:, sum_lane:sum_lane + 1]

    outs = []

    def finish(j, h, p):
        outs.append(weighted(j, h, *p))
        if h == 1:
            lo, hi = j * blk, (j + 1) * blk
            o_ref[0, lo:hi, :] = jnp.where(lo_half, outs[0], outs[1]).astype(o_ref.dtype)
            outs.clear()

    ahead = scores(*items[0])
    behind = None
    for k, (j, h) in enumerate(items):
        current = ahead
        if k + 1 < len(items):
            ahead = scores(*items[k + 1])
        p = probs(j, *current)
        if behind:
            finish(*behind)
        behind = (j, h, p)
    finish(*behind)


def _moba_prompt(q, kb, vb, km, tables):
    b, s, _ = q.shape
    nb = s // MOBA_BLOCK
    assert nb == 8, "block-indicator columns are laid out for 8 MoBA blocks"
    npair = N_HEADS_A // 2
    seq_spec = pl.BlockSpec((1, s, LANES), lambda bi, p: (bi, 0, p))
    return pl.pallas_call(
        functools.partial(_moba_prompt_body, nb=nb),
        grid=(b, npair),
        in_specs=[seq_spec, pl.BlockSpec((1, LANES, s), lambda bi, p: (bi, p, 0)), seq_spec,
                  pl.BlockSpec((1, nb, LANES), lambda bi, p: (bi, 0, p)),
                  pl.BlockSpec((1, LANES, s), lambda bi, p: (p, 0, 0))],
        out_specs=seq_spec,
        out_shape=jax.ShapeDtypeStruct((b, s, W_A), BF16),
        scratch_shapes=[pltpu.VMEM((s, LANES), BF16)] * 2 + [pltpu.VMEM((LANES, s), BF16)] * 2
        + [pltpu.VMEM((s, LANES), BF16)] * 2,
        compiler_params=_params("parallel", "parallel"),
        name="moba_prompt",
    )(q, kb, vb, km, tables)


def _moba_sample_stages(pt_ref, q_ref, kn_ref, vn_ref, kt_hbm, vt_hbm, o_ref, buf, sem, s_ref,
                        p_ref, *, layer, n_real, past_len):
    b = pl.program_id(0)
    n_seq = pl.num_programs(0)
    n_pages = s_ref.shape[0]
    ring = PAGE_SLOTS
    rows = N_HEADS_A * n_real

    def page_copy(hbm, seq, pg):
        slot = pg % ring
        return pltpu.make_async_copy(hbm.at[layer, pt_ref[seq, pg]], buf.at[slot], sem.at[slot])

    @pl.when(b == 0)
    def _():
        for pg in range(ring):
            page_copy(kt_hbm, b, pg).start()

    rowi = lax.broadcasted_iota(jnp.int32, (rows, W_A), 0)
    lanei = lax.broadcasted_iota(jnp.int32, (rows, W_A), 1)
    head_mask = (lanei >> 6) == (rowi & (N_HEADS_A - 1))
    q = q_ref[0]
    q_rep = jnp.concatenate(
        [jnp.broadcast_to(q[t:t + 1, :], (N_HEADS_A, W_A)) for t in range(n_real)], axis=0)
    q_hi, q_lo = _split_bf16(jnp.where(head_mask, q_rep, 0.0))
    q2 = jnp.concatenate([q_hi, q_lo], axis=0)

    def raw_scores(keys_t=None, keys=None):
        s2 = _mm(q2, keys_t) if keys_t is not None else _nt(q2, keys)
        return s2[:rows] + s2[rows:]

    lane = lax.broadcasted_iota(jnp.int32, (1, LANES), 1)

    def start_group(hbm, seq, first):
        for pg in range(first, first + PAGE_GROUP):
            page_copy(hbm, seq, pg).start()

    def wait_group(hbm, first):
        for pg in range(first, first + PAGE_GROUP):
            page_copy(hbm, b, pg).wait()

    st = {"gsum": jnp.zeros((rows, LANES), F32)}

    def k_group(first):
        gsum = st["gsum"]
        for pg in range(first, first + PAGE_GROUP):
            s = raw_scores(keys_t=buf[pg % ring].astype(BF16))
            s_ref[pg] = s
            gsum = jnp.where(lane == pg, jnp.sum(s, axis=-1, keepdims=True), gsum)
        st["gsum"] = gsum

    def k_refill(first):
        if first + ring < n_pages:
            start_group(kt_hbm, b, first + ring)
        else:
            start_group(vt_hbm, b, first + ring - n_pages)

    def select_and_softmax():
        gsum = st["gsum"]
        gate = (gsum + pltpu.roll(gsum, LANES - 1, 1)) * (1.0 / MOBA_BLOCK)
        valid = ((lane & 1) == 0) & (lane < n_pages)
        sel = _rank_select(gate, lane, valid, range(0, n_pages, 2))

        rcol = lax.broadcasted_iota(jnp.int32, (rows, 1), 0)
        slope = jnp.exp2(-((rcol & (N_HEADS_A - 1)) + 1).astype(F32))
        qpos = past_len + (rcol >> 3)

        m_lanes = jnp.full((rows, LANES), NEG_INF, F32)
        for blk in range(n_pages // 2):
            chosen = sel[:, 2 * blk:2 * blk + 1] > 0.5
            for pg in (2 * blk, 2 * blk + 1):
                kpos = pg * LANES + lane
                s = s_ref[pg] - slope * (qpos - kpos).astype(F32)
                s = jnp.where(chosen, s, NEG_INF)
                s_ref[pg] = s
                m_lanes = jnp.maximum(m_lanes, s)
        tnew = lax.broadcasted_iota(jnp.int32, (1, SAMPLE_ROWS), 1)
        dist = (rcol >> 3) - tnew
        s_own = raw_scores(keys=kn_ref[0]) - slope * dist.astype(F32)
        s_own = jnp.where(dist >= 0, s_own, NEG_INF)
        m = jnp.maximum(jnp.max(m_lanes, axis=-1, keepdims=True),
                        jnp.max(s_own, axis=-1, keepdims=True))

        l_lanes = jnp.zeros((rows, LANES), F32)
        for pg in range(n_pages):
            p = jnp.exp(s_ref[pg] - m)
            l_lanes = l_lanes + p
            p_ref[pg] = p.astype(BF16)
        p_own = jnp.exp(s_own - m)
        st["l"] = jnp.sum(l_lanes, axis=-1, keepdims=True) + jnp.sum(p_own, axis=-1, keepdims=True)
        st["acc"] = _mm(p_own.astype(BF16), vn_ref[0])

    def v_group(first):
        acc = st["acc"]
        for pg in range(first, first + PAGE_GROUP):
            acc = acc + _nt(p_ref[pg], buf[pg % ring].astype(BF16))
        st["acc"] = acc

    def v_refill(first):
        if first + ring < n_pages:
            start_group(vt_hbm, b, first + ring)
        else:
            @pl.when(b + 1 < n_seq)
            def _():
                start_group(kt_hbm, b + 1, first + ring - n_pages)

    def finish():
        o = jnp.where(head_mask, st["acc"] / st["l"], 0.0)
        out_rows = [jnp.sum(o[N_HEADS_A * t:N_HEADS_A * (t + 1), :], axis=0, keepdims=True)
                    for t in range(n_real)]
        out_rows.append(jnp.zeros((SAMPLE_ROWS - n_real, W_A), F32))
        o_ref[0] = jnp.concatenate(out_rows, axis=0).astype(o_ref.dtype)

    part = functools.partial
    nothing = lambda: None
    groups = range(0, n_pages, PAGE_GROUP)
    return ([(part(wait_group, kt_hbm, f), part(k_group, f), part(k_refill, f)) for f in groups]
            + [(nothing, select_and_softmax, nothing)]
            + [(part(wait_group, vt_hbm, f), part(v_group, f), part(v_refill, f)) for f in groups]
            + [(nothing, finish, nothing)])


def _ffn_moba_sample_body(pt_ref, x_ref, g_ref, w1_ref, w3_ref, w2_ref, q_ref, kn_ref, vn_ref,
                          kt_hbm, vt_hbm, y_ref, o_ref, buf, sem, s_ref, p_ref,
                          *, layer, n_real, past_len):
    stages = _moba_sample_stages(pt_ref, q_ref, kn_ref, vn_ref, kt_hbm, vt_hbm, o_ref, buf, sem,
                                 s_ref, p_ref, layer=layer, n_real=n_real, past_len=past_len)
    x = x_ref[...]
    h = _rms(x, g_ref[...]).astype(BF16)
    n_chunks = D_FF // FFN_CHUNK
    assert len(stages) <= n_chunks
    acc = jnp.zeros(x.shape, F32)
    for c in range(n_chunks):
        cols = slice(c * FFN_CHUNK, (c + 1) * FFN_CHUNK)
        a = _mm(h, w1_ref[:, cols])
        gate = _mm(h, w3_ref[:, cols])
        act = (a * jax.nn.sigmoid(a) * gate).astype(BF16)
        acc = acc + _mm(act, w2_ref[cols, :])
        if c < len(stages):
            for part in stages[c]:
                part()
    y_ref[...] = x + 0.5 * acc


def _ffn_moba_sample(x, layer, g, w1, w3, w2, tm, page_table, q, kn, vn, cache_kt, cache_vt, n_real):
    m = x.shape[0]
    bd, n_pages = page_table.shape
    page = cache_kt.shape[-1]
    assert m // tm == bd, "one ffn row tile per sample sequence"
    assert page == LANES and MOBA_BLOCK == 2 * page
    assert PAGE_SLOTS <= n_pages <= LANES and n_pages % PAGE_GROUP == 0 == PAGE_SLOTS % PAGE_GROUP
    rows = N_HEADS_A * n_real

    def layer_spec(shape):
        return pl.BlockSpec((None,) + shape, lambda i, pt: (layer, 0, 0), pipeline_mode=pl.Buffered(1))

    row_spec = pl.BlockSpec((tm, D_MODEL), lambda i, pt: (i, 0))
    tok_spec = pl.BlockSpec((1, SAMPLE_ROWS, W_A), lambda i, pt: (i, 0, 0))
    hbm_spec = pl.BlockSpec(memory_space=pl.ANY)
    grid_spec = pltpu.PrefetchScalarGridSpec(
        num_scalar_prefetch=1,
        grid=(bd,),
        in_specs=[row_spec, layer_spec((1, D_MODEL)), layer_spec((D_MODEL, D_FF)),
                  layer_spec((D_MODEL, D_FF)), layer_spec((D_FF, D_MODEL)),
                  tok_spec, tok_spec, tok_spec, hbm_spec, hbm_spec],
        out_specs=[row_spec, tok_spec],
        scratch_shapes=[pltpu.VMEM((PAGE_SLOTS, W_A, page), F32),
                        pltpu.SemaphoreType.DMA((PAGE_SLOTS,)),
                        pltpu.VMEM((n_pages, rows, LANES), F32),
                        pltpu.VMEM((n_pages, rows, LANES), BF16)],
    )
    return pl.pallas_call(
        functools.partial(_ffn_moba_sample_body, layer=layer, n_real=n_real,
                          past_len=n_pages * page),
        grid_spec=grid_spec,
        out_shape=[jax.ShapeDtypeStruct((m, D_MODEL), F32),
                   jax.ShapeDtypeStruct((bd, SAMPLE_ROWS, W_A), BF16)],
        compiler_params=_params("arbitrary"),
        name="ffn_moba_sample",
    )(page_table, x, g, w1, w3, w2, q, kn, vn, cache_kt, cache_vt)


def _log_decay_rows():
    log_decay = np.log(1.0 - 2.0 ** (-5.0 - np.arange(N_HEADS_R, dtype=np.float32)))
    rows = np.broadcast_to(log_decay.astype(np.float32).reshape(N_HEADS_R, 1, 1),
                           (N_HEADS_R, 1, LANES))
    return np.ascontiguousarray(rows)


def _retention_body(lg_ref, q_ref, k_ref, v_ref, gr_ref, gain_ref, bias_ref, s0_ref,
                    o_ref, s_out_ref, *, chunk, n_real, n_chunks, heads):
    hd = KEY_DIM_R
    ii = lax.broadcasted_iota(jnp.int32, (chunk, chunk), 0)
    jj = lax.broadcasted_iota(jnp.int32, (chunk, chunk), 1)
    diff = (ii - jj).astype(F32)
    icol = lax.broadcasted_iota(jnp.int32, (chunk, 1), 0).astype(F32)
    for h in range(heads):
        cols = slice(h * hd, (h + 1) * hd)
        lg = lg_ref[h][:, 0:1]
        dmat = jnp.where(diff >= 0, jnp.exp(lg * jnp.maximum(diff, 0.0)), 0.0)
        qdec = jnp.exp(lg * (icol + 1.0))
        kdec = jnp.where(icol < n_real, jnp.exp(lg * (n_real - 1.0 - icol)), 0.0)
        chunk_decay = jnp.exp(lg * float(n_real))
        gain = gain_ref[:, cols]
        bias = bias_ref[:, cols]
        state = s0_ref[0, h]
        for c in range(n_chunks):
            rows = slice(c * chunk, (c + 1) * chunk)
            qc = q_ref[0, rows, cols]
            kc = k_ref[0, rows, cols]
            vc = v_ref[0, rows, cols]
            inner = _nt(qc, kc) * dmat
            o = _mm(inner.astype(BF16), vc) + _mm(qc, state.astype(BF16)) * qdec
            kd = (kc.astype(F32) * kdec).astype(BF16)
            state = chunk_decay * state + _tn(kd, vc)
            mu = jnp.mean(o, axis=-1, keepdims=True)
            d = o - mu
            var = jnp.mean(d * d, axis=-1, keepdims=True)
            y = d * lax.rsqrt(var + EPS) * gain + bias
            gr = gr_ref[0, rows, cols].astype(F32)
            o_ref[0, rows, cols] = (y * (gr * jax.nn.sigmoid(gr))).astype(o_ref.dtype)
        s_out_ref[0, h] = state


def _retention(lg, q, k, v, gr, gain, bias, layer, state0, state_offset, chunk, n_real, heads):
    b, s, _ = q.shape
    hd = KEY_DIM_R
    width = heads * hd
    seq_spec = pl.BlockSpec((1, s, width), lambda bi, g: (bi, 0, g))
    vec_spec = pl.BlockSpec((None, 1, width), lambda bi, g: (layer, 0, g))
    st_spec = pl.BlockSpec((1, heads, hd, hd), lambda bi, g: (bi, g, 0, 0))
    st_in_spec = pl.BlockSpec((1, heads, hd, hd), lambda bi, g: (state_offset + bi, g, 0, 0))
    return pl.pallas_call(
        functools.partial(_retention_body, chunk=chunk, n_real=n_real, n_chunks=s // chunk,
                          heads=heads),
        grid=(b, N_HEADS_R // heads),
        in_specs=[pl.BlockSpec((heads, 1, LANES), lambda bi, g: (g, 0, 0)),
                  seq_spec, seq_spec, seq_spec, seq_spec, vec_spec, vec_spec, st_in_spec],
        out_specs=[seq_spec, st_spec],
        out_shape=[jax.ShapeDtypeStruct((b, s, W_R), BF16),
                   jax.ShapeDtypeStruct((b, N_HEADS_R, hd, hd), F32)],
        compiler_params=_params("parallel", "parallel"),
        name="retention",
    )(lg, q, k, v, gr, gain, bias, state0)


def kernel(x_prompt, x_sample, cache_k, cache_v, state_ret, page_table, g_ffn1, w1_ffn1, w3_ffn1,
           w2_ffn1, g_mix, w_in, g_q, g_k, gn_gain, gn_bias, w_pa, w_pr, w_o, g_ffn2, w1_ffn2,
           w3_ffn2, w2_ffn2):
    b, s, _ = x_prompt.shape
    bd, sd, _ = x_sample.shape
    depth = w_in.shape[0]
    n_pool, page = cache_k.shape[1], cache_k.shape[2]
    mp, ms = b * s, bd * SAMPLE_ROWS
    tm_p, tm_s = 512, ms

    xp = x_prompt.reshape(mp, D_MODEL)
    xs = jnp.pad(x_sample, ((0, 0), (0, SAMPLE_ROWS - sd), (0, 0))).reshape(ms, D_MODEL)

    cache_kt = jnp.transpose(cache_k, (0, 1, 3, 4, 2)).reshape(depth, n_pool, W_A, page)
    cache_vt = jnp.transpose(cache_v, (0, 1, 3, 4, 2)).reshape(depth, n_pool, W_A, page)

    tables = _moba_tables(s)
    lg = _log_decay_rows()
    head_ones = np.kron(np.eye(N_HEADS_A // 2, dtype=np.float32),
                        np.ones((HEAD_DIM_A, HEAD_DIM_A), np.float32)).astype(BF16)
    zero_state = jnp.zeros((b, N_HEADS_R, KEY_DIM_R, VAL_DIM_R), F32)
    states_in = state_ret.reshape(depth * bd, N_HEADS_R, KEY_DIM_R, VAL_DIM_R)

    vec = lambda a: a.reshape(depth, 1, -1)
    g1, gm, g2, gain, bias = vec(g_ffn1), vec(g_mix), vec(g_ffn2), vec(gn_gain), vec(gn_bias)
    gq, gk = vec(jnp.tile(g_q, (1, N_HEADS_A))), vec(jnp.tile(g_k, (1, N_HEADS_A)))
    w1a, w3a, w2a = w1_ffn1.astype(BF16), w3_ffn1.astype(BF16), w2_ffn1.astype(BF16)
    w1b, w3b, w2b = w1_ffn2.astype(BF16), w3_ffn2.astype(BF16), w2_ffn2.astype(BF16)
    win, wpa, wpr, wo = w_in.astype(BF16), w_pa.astype(BF16), w_pr.astype(BF16), w_o.astype(BF16)

    kv_p = kv_s = None
    sp_l, ss_l = [], []
    for l in range(depth):
        xp = _ffn(xp, l, g1, w1a, w3a, w2a, tm_p)
        q, kf, kb, vf, vb, qr, kr, vr, gr, ga, gb, km = _inproj(
            xp, l, depth, gm, win, gq, gk, head_ones, tm_p, kv_p, seq_len=s)
        kv_p = (kf, vf)
        r3 = lambda a: a.reshape(b, s, a.shape[-1])
        oa = _moba_prompt(r3(q), kb, r3(vb), km.reshape(b, s // MOBA_BLOCK, W_A), tables)
        o_r, st = _retention(lg, r3(qr), r3(kr), r3(vr), r3(gr), gain, bias, l, zero_state, 0,
                             RET_CHUNK_PROMPT, RET_CHUNK_PROMPT, RET_HEADS_PROMPT)
        xp = _outproj(oa.reshape(mp, W_A), o_r.reshape(mp, W_R), ga, gb, xp, l, wpa, wpr, wo, tm_p)
        sp_l.append(st)

        xs = _ffn_few_rows(xs, l, g1, w1a, w3a, w2a)
        q, kf, kb, vf, vb, qr, kr, vr, gr, ga, gb = _inproj(
            xs, l, depth, gm, win, gq, gk, head_ones, tm_s, kv_s)
        kv_s = (kf, vf)
        r3 = lambda a: a.reshape(bd, SAMPLE_ROWS, a.shape[-1])
        xp, oa = _ffn_moba_sample(xp, l, g2, w1b, w3b, w2b, tm_p, page_table, r3(q), r3(kb), r3(vb),
                                  cache_kt, cache_vt, sd)
        o_r, st = _retention(lg, r3(qr), r3(kr), r3(vr), r3(gr), gain, bias, l, states_in, l * bd,
                             SAMPLE_ROWS, sd, N_HEADS_R)
        xs = _outproj(oa.reshape(ms, W_A), o_r.reshape(ms, W_R), ga, gb, xs, l, wpa, wpr, wo, tm_s)
        xs = _ffn_few_rows(xs, l, g2, w1b, w3b, w2b)
        ss_l.append(st)

    y_prompt = xp.reshape(b, s, D_MODEL)
    y_sample = xs.reshape(bd, SAMPLE_ROWS, D_MODEL)[:, :sd]
    seq_minor = lambda a: a.reshape(depth, b, N_HEADS_A, HEAD_DIM_A, s).transpose(0, 1, 4, 2, 3)
    k_prompt, v_prompt = seq_minor(kv_p[0]), seq_minor(kv_p[1])
    heads5 = lambda a: a.reshape(depth, bd, SAMPLE_ROWS, N_HEADS_A, HEAD_DIM_A)[:, :, :sd]
    k_sample, v_sample = heads5(kv_s[0]), heads5(kv_s[1])
    return (y_prompt, y_sample, k_prompt, v_prompt, jnp.stack(sp_l),
            k_sample, v_sample, jnp.stack(ss_l))
```
